```python
import math
import jax
import jax.numpy as jnp
from jax import lax
import numpy as np

D_MODEL = 4096
BATCH = 4
SEQ = 4096
DEPTH = 4

CTX_LEN = 256
GRID_W = 64

MIX_W = D_MODEL // 4
N_IN_SPLITS = 7
IN_WIDTH = N_IN_SPLITS * MIX_W
N_BRANCH = 3

LRU_BLOCKS = 8
LRU_BW = MIX_W // LRU_BLOCKS
LRU_C = 8.0
CONV_W = 4
CONV_LEFT = (CONV_W - 1) // 2

RET_HEAD_DIM = 128
RET_HEADS = MIX_W // RET_HEAD_DIM
RET_CHUNK = 128
ROPE_BASE = 10000.0

S5_IN = 16
S5_GROUPS = MIX_W // S5_IN
S5_STATE = 64
S5_DT_MIN = 1e-3
S5_DT_MAX = 1e-1

MOD_RANK = D_MODEL // 16
N_MOD = 6

N_EXPERTS = 32
TOP_K = 4
EXPERT_FF = D_MODEL // 16
SWIGLU_LIMIT = 7.0
SWIGLU_ALPHA = 1.702

NORM_EPS = 1e-6

kernel_name = 'hybrid_lru_retention_s5_moe_dit_block'


def rms_norm(x, g):
    xf = x.astype(jnp.float32)
    y = xf * lax.rsqrt(jnp.mean(jnp.square(xf), axis=-1, keepdims=True) + NORM_EPS)
    return (y * g.astype(jnp.float32)).astype(x.dtype)


def dwconv_centred(x, w, b):
    n = x.shape[1]
    xp = jnp.pad(x, ((0, 0), (CONV_LEFT, CONV_W - 1 - CONV_LEFT), (0, 0)))
    y = b + xp[:, 0:n] * w[0]
    for j in range(1, CONV_W):
        y = y + xp[:, j:j + n] * w[j]
    return y


def lru_gates(xc, w_a, b_a, w_x, b_x, lam):
    bsz, n, wd = xc.shape
    xb = xc.reshape(bsz, n, LRU_BLOCKS, LRU_BW)
    r = jax.nn.sigmoid(jnp.einsum('blhi,hij->blhj', xb, w_a).reshape(bsz, n, wd) + b_a)
    i = jax.nn.sigmoid(jnp.einsum('blhi,hij->blhj', xb, w_x).reshape(bsz, n, wd) + b_x)
    log_a = -LRU_C * r * jax.nn.softplus(-lam)
    a = jnp.exp(log_a)
    beta = jnp.sqrt(-jnp.expm1(2.0 * log_a))
    return a, beta * (i * xc)


def lru_scan(a, bx, h0, reverse):
    def step(h, ab):
        h = ab[0] * h + ab[1]
        return h, h
    h_last, hs = lax.scan(step, h0, (jnp.swapaxes(a, 0, 1), jnp.swapaxes(bx, 0, 1)), reverse=reverse)
    return jnp.swapaxes(hs, 0, 1), h_last


def rglru_branch(xc, xl, gc, gl, conv_w, conv_b, w_a, b_a, w_x, b_x, lam):
    f32 = jnp.float32
    xc = dwconv_centred(xc, conv_w, conv_b).astype(f32)
    xl = dwconv_centred(xl, conv_w, conv_b).astype(f32)
    h0 = jnp.zeros((xc.shape[0], xc.shape[2]), f32)
    yc = jnp.zeros_like(xc)
    yl = jnp.zeros_like(xl)
    for d in range(2):
        rev = d == 1
        a, bx = lru_gates(xc, w_a[d], b_a[d], w_x[d], b_x[d], lam[d])
        hs, h_ctx = lru_scan(a, bx, h0, rev)
        yc = yc + hs
        a, bx = lru_gates(xl, w_a[d], b_a[d], w_x[d], b_x[d], lam[d])
        hs, _ = lru_scan(a, bx, h_ctx, rev)
        yl = yl + hs
    out_c = jax.nn.gelu(gc.astype(f32)) * yc
    out_l = jax.nn.gelu(gl.astype(f32)) * yl
    return out_c.astype(gc.dtype), out_l.astype(gl.dtype)


def to_heads(t):
    bsz, n, _ = t.shape
    return t.astype(jnp.float32).reshape(bsz, n, RET_HEADS, RET_HEAD_DIM).transpose(0, 2, 1, 3)


def from_heads(t):
    bsz, h, n, d = t.shape
    return t.transpose(0, 2, 1, 3).reshape(bsz, n, h * d)


def rope_axis(x, pos):
    half = x.shape[-1] // 2
    freqs = ROPE_BASE ** (-jnp.arange(half, dtype=jnp.float32) / half)
    ang = pos.astype(jnp.float32)[:, None] * freqs
    cos, sin = jnp.cos(ang), jnp.sin(ang)
    x1, x2 = x[..., :half], x[..., half:]
    return jnp.concatenate([x1 * cos - x2 * sin, x1 * sin + x2 * cos], axis=-1)


def rope_2d(x, row, col):
    h = x.shape[-1] // 2
    return jnp.concatenate([rope_axis(x[..., :h], row), rope_axis(x[..., h:], col)], axis=-1)


def retention_chunkwise(q, k, v, log_g, s0):
    bsz, nh, n, d = q.shape
    nc = n // RET_CHUNK
    q = q.reshape(bsz, nh, nc, RET_CHUNK, d)
    k = k.reshape(bsz, nh, nc, RET_CHUNK, d)
    v = v.reshape(bsz, nh, nc, RET_CHUNK, d)
    pos = jnp.arange(RET_CHUNK, dtype=jnp.float32)
    lg = log_g[:, None, None, None]
    diff = pos[:, None] - pos[None, :]
    intra_decay = jnp.where(diff >= 0, jnp.exp(lg * jnp.maximum(diff, 0.0)), 0.0)
    scores = jnp.einsum('bhnid,bhnjd->bhnij', q, k) * intra_decay
    intra = jnp.einsum('bhnij,bhnjd->bhnid', scores, v)
    k_dec = k * jnp.exp(lg * (RET_CHUNK - 1 - pos)[:, None])
    chunk_state = jnp.einsum('bhncd,bhnce->bhnde', k_dec, v)
    chunk_decay = jnp.exp(log_g * RET_CHUNK)[None, :, None, None]

    def step(s, cs):
        return chunk_decay * s + cs, s
    s_last, s_starts = lax.scan(step, s0, jnp.moveaxis(chunk_state, 2, 0))
    s_starts = jnp.moveaxis(s_starts, 0, 2)
    q_dec = q * jnp.exp(lg * (pos + 1.0)[:, None])
    inter = jnp.einsum('bhncd,bhnde->bhnce', q_dec, s_starts)
    return (intra + inter).reshape(bsz, nh, n, d), s_last


def retention_branch(qc, kc, vc, gc, ql, kl, vl, gl, row, col, decay_logit):
    f32 = jnp.float32
    lg = jax.nn.log_sigmoid(decay_logit.astype(f32))
    k_scale = RET_HEAD_DIM ** -0.5
    qc_h, kc_h, vc_h = to_heads(qc), to_heads(kc) * k_scale, to_heads(vc)
    ql_h = rope_2d(to_heads(ql), row, col)
    kl_h = rope_2d(to_heads(kl), row, col) * k_scale
    vl_h = to_heads(vl)
    bsz = qc_h.shape[0]
    s0 = jnp.zeros((bsz, RET_HEADS, RET_HEAD_DIM, RET_HEAD_DIM), f32)

    def flip(t):
        return jnp.flip(t, axis=2)
    oc_f, sc_f = retention_chunkwise(qc_h, kc_h, vc_h, lg[0], s0)
    ol_f, _ = retention_chunkwise(ql_h, kl_h, vl_h, lg[0], sc_f)
    oc_b, sc_b = retention_chunkwise(flip(qc_h), flip(kc_h), flip(vc_h), lg[1], s0)
    ol_b, _ = retention_chunkwise(flip(ql_h), flip(kl_h), flip(vl_h), lg[1], sc_b)
    oc = oc_f + flip(oc_b)
    ol = ol_f + flip(ol_b)

    def head_norm(o):
        return o * lax.rsqrt(jnp.mean(jnp.square(o), axis=-1, keepdims=True) + NORM_EPS)
    yc = from_heads(head_norm(oc)) * jax.nn.silu(gc.astype(f32))
    yl = from_heads(head_norm(ol)) * jax.nn.silu(gl.astype(f32))
    return yc.astype(gc.dtype), yl.astype(gl.dtype)


def s5_discretise(a_re, a_im, log_dt, b_re, b_im):
    f32 = jnp.float32
    lam_re = jnp.minimum(a_re.astype(f32), -1e-4)
    lam_im = a_im.astype(f32)
    dt = jnp.exp(log_dt.astype(f32))[:, None]
    z_re, z_im = lam_re * dt, lam_im * dt
    mag = jnp.exp(z_re)
    ab_re, ab_im = mag * jnp.cos(z_im), mag * jnp.sin(z_im)
    den = jnp.square(lam_re) + jnp.square(lam_im)
    n_re = ab_re - 1.0
    co_re = (n_re * lam_re + ab_im * lam_im) / den
    co_im = (ab_im * lam_re - n_re * lam_im) / den
    b_re = b_re.astype(f32)
    b_im = b_im.astype(f32)
    bb_re = co_re[..., None] * b_re - co_im[..., None] * b_im
    bb_im = co_re[..., None] * b_im + co_im[..., None] * b_re
    return ab_re, ab_im, bb_re, bb_im


def s5_combine(e1, e2):
    a1r, a1i, b1r, b1i = e1
    a2r, a2i, b2r, b2i = e2
    return (a1r * a2r - a1i * a2i,
            a1r * a2i + a1i * a2r,
            a2r * b1r - a2i * b1i + b2r,
            a2r * b1i + a2i * b1r + b2i)


def s5_scan(u, ab_re, ab_im, bb_re, bb_im, h0_re, h0_im, reverse):
    n = u.shape[1]
    bu_re = jnp.einsum('blgi,gpi->lbgp', u, bb_re)
    bu_im = jnp.einsum('blgi,gpi->lbgp', u, bb_im)
    a_re = jnp.broadcast_to(ab_re[None, None], (n, 1) + ab_re.shape)
    a_im = jnp.broadcast_to(ab_im[None, None], (n, 1) + ab_im.shape)
    acr, aci, hr, hi = lax.associative_scan(s5_combine, (a_re, a_im, bu_re, bu_im), reverse=reverse, axis=0)
    hr = hr + acr * h0_re - aci * h0_im
    hi = hi + acr * h0_im + aci * h0_re
    return hr, hi


def s5_readout(hr, hi, c_re, c_im):
    y = jnp.einsum('lbgp,gip->blgi', hr, c_re) - jnp.einsum('lbgp,gip->blgi', hi, c_im)
    return y.reshape(y.shape[0], y.shape[1], MIX_W)


def s5_branch(uc, ul, a_re, a_im, log_dt, b_re, b_im, c_re, c_im, d_skip, w_glu, b_glu):
    f32 = jnp.float32
    bsz = uc.shape[0]

    def groups(u):
        return u.astype(f32).reshape(u.shape[0], u.shape[1], S5_GROUPS, S5_IN)
    gc, gl = groups(uc), groups(ul)
    yc = d_skip.astype(f32) * uc.astype(f32)
    yl = d_skip.astype(f32) * ul.astype(f32)
    zero = jnp.zeros((bsz, S5_GROUPS, S5_STATE), f32)
    for d in range(2):
        rev = d == 1
        end = 0 if rev else -1
        ab_re, ab_im, bb_re, bb_im = s5_discretise(a_re[d], a_im[d], log_dt[d], b_re[d], b_im[d])
        cr, ci = c_re[d].astype(f32), c_im[d].astype(f32)
        hr, hi = s5_scan(gc, ab_re, ab_im, bb_re, bb_im, zero, zero, rev)
        yc = yc + s5_readout(hr, hi, cr, ci)
        hr, hi = s5_scan(gl, ab_re, ab_im, bb_re, bb_im, hr[end], hi[end], rev)
        yl = yl + s5_readout(hr, hi, cr, ci)

    def glu(y):
        y = jax.nn.gelu(y)
        return y * jax.nn.sigmoid(y @ w_glu + b_glu)
    return glu(yc).astype(uc.dtype), glu(yl).astype(ul.dtype)


def hybrid_mixer(hc, hl, row, col, w_in, conv_w, conv_b, lru_w_a, lru_b_a, lru_w_x, lru_b_x, lru_lam,
                 ret_decay, s5_a_re, s5_a_im, s5_log_dt, s5_b_re, s5_b_im, s5_c_re, s5_c_im, s5_d,
                 s5_w_glu, s5_b_glu, w_branch, w_gate, b_gate, w_out, with_ctx_out):
    uc = jnp.split(hc @ w_in, N_IN_SPLITS, axis=-1)
    ul = jnp.split(hl @ w_in, N_IN_SPLITS, axis=-1)
    ya = rglru_branch(uc[0], ul[0], uc[1], ul[1], conv_w, conv_b, lru_w_a, lru_b_a, lru_w_x, lru_b_x, lru_lam)
    yb = retention_branch(uc[2], uc[3], uc[4], uc[5], ul[2], ul[3], ul[4], ul[5], row, col, ret_decay)
    yc = s5_branch(uc[6], ul[6], s5_a_re, s5_a_im, s5_log_dt, s5_b_re, s5_b_im, s5_c_re, s5_c_im,
                   s5_d, s5_w_glu, s5_b_glu)
    branches = (ya, yb, yc)

    def merge(h, side):
        m = 0.0
        for b in range(N_BRANCH):
            gate = jax.nn.sigmoid(h @ w_gate[b] + b_gate[b])
            m = m + gate * (branches[b][side] @ w_branch[b])
        return m @ w_out
    ml = merge(hl, 1)
    mc = merge(hc, 0) if with_ctx_out else None
    return mc, ml


def moe_ffn(h, router_w, router_b, w1, b1, w2, b2):
    f32 = jnp.float32
    shp = h.shape
    t = h.reshape(-1, shp[-1])
    logits = (t @ router_w + router_b).astype(f32)
    top_v, top_i = lax.top_k(logits, TOP_K)
    probs = jax.nn.softmax(top_v, axis=-1)
    comb = jnp.einsum('tk,tke->te', probs, jax.nn.one_hot(top_i, N_EXPERTS, dtype=f32))
    out = jnp.zeros(t.shape, f32)
    for e in range(N_EXPERTS):
        hu = t @ w1[e] + b1[e]
        gate = jnp.minimum(hu[:, :EXPERT_FF], SWIGLU_LIMIT)
        up = jnp.clip(hu[:, EXPERT_FF:], -SWIGLU_LIMIT, SWIGLU_LIMIT)
        act = gate * jax.nn.sigmoid(SWIGLU_ALPHA * gate) * (up + 1.0)
        out = out + comb[:, e:e + 1] * (act @ w2[e] + b2[e])
    return out.reshape(shp).astype(h.dtype)


def setup_inputs(seed: int = 0) -> dict:
    key = jax.random.key(seed)
    ks = iter(jax.random.split(key, 48))
    f32 = jnp.float32

    def nrm(shape, scale):
        return jax.random.normal(next(ks), shape, f32) * scale

    def unif(shape, lo, hi):
        return jax.random.uniform(next(ks), shape, f32, lo, hi)

    x = nrm((BATCH, SEQ, D_MODEL), 1.0)
    c = nrm((BATCH, D_MODEL), 1.0)
    ctx = nrm((BATCH, CTX_LEN, D_MODEL), 1.0)
    c_ctx = nrm((D_MODEL,), 1.0)
    mod_w_a = nrm((DEPTH, D_MODEL, MOD_RANK), D_MODEL ** -0.5)
    mod_w_b = nrm((DEPTH, MOD_RANK, N_MOD * D_MODEL), 0.5 * MOD_RANK ** -0.5)
    mod_b = nrm((DEPTH, N_MOD * D_MODEL), 0.02)
    norm_mix_g = 1.0 + nrm((DEPTH, D_MODEL), 0.01)
    norm_ffn_g = 1.0 + nrm((DEPTH, D_MODEL), 0.01)
    w_in = nrm((DEPTH, D_MODEL, IN_WIDTH), D_MODEL ** -0.5)
    conv_w = nrm((DEPTH, CONV_W, MIX_W), CONV_W ** -0.5)
    conv_b = nrm((DEPTH, MIX_W), 0.02)
    lru_w_a = nrm((DEPTH, 2, LRU_BLOCKS, LRU_BW, LRU_BW), LRU_BW ** -0.5)
    lru_b_a = nrm((DEPTH, 2, MIX_W), 0.02)
    lru_w_x = nrm((DEPTH, 2, LRU_BLOCKS, LRU_BW, LRU_BW), LRU_BW ** -0.5)
    lru_b_x = nrm((DEPTH, 2, MIX_W), 0.02)
    a0 = unif((DEPTH, 2, MIX_W), 0.9, 0.999)
    s = a0 ** (1.0 / LRU_C)
    lru_lam = jnp.log(s) - jnp.log1p(-s)
    hidx = jnp.arange(RET_HEADS, dtype=f32)
    gam = 1.0 - jnp.exp2(-5.0 - hidx)
    ret_decay = (jnp.log(gam) - jnp.log1p(-gam)) + nrm((DEPTH, 2, RET_HEADS), 0.05)
    s5_a_re = -0.5 + nrm((DEPTH, 2, S5_GROUPS, S5_STATE), 0.01)
    s5_a_im = jnp.pi * jnp.arange(S5_STATE, dtype=f32) + nrm((DEPTH, 2, S5_GROUPS, S5_STATE), 0.01)
    s5_log_dt = unif((DEPTH, 2, S5_GROUPS), math.log(S5_DT_MIN), math.log(S5_DT_MAX))
    s5_b_re = nrm((DEPTH, 2, S5_GROUPS, S5_STATE, S5_IN), (2.0 * S5_IN) ** -0.5)
    s5_b_im = nrm((DEPTH, 2, S5_GROUPS, S5_STATE, S5_IN), (2.0 * S5_IN) ** -0.5)
    s5_c_re = nrm((DEPTH, 2, S5_GROUPS, S5_IN, S5_STATE), S5_STATE ** -0.5)
    s5_c_im = nrm((DEPTH, 2, S5_GROUPS, S5_IN, S5_STATE), S5_STATE ** -0.5)
    s5_d = nrm((DEPTH, MIX_W), 1.0)
    s5_w_glu = nrm((DEPTH, MIX_W, MIX_W), MIX_W ** -0.5)
    s5_b_glu = nrm((DEPTH, MIX_W), 0.02)
    w_branch = nrm((DEPTH, N_BRANCH, MIX_W, D_MODEL), MIX_W ** -0.5)
    w_gate = nrm((DEPTH, N_BRANCH, D_MODEL, D_MODEL), D_MODEL ** -0.5)
    b_gate = nrm((DEPTH, N_BRANCH, D_MODEL), 0.02)
    w_out = nrm((DEPTH, D_MODEL, D_MODEL), D_MODEL ** -0.5)
    router_w = nrm((DEPTH, D_MODEL, N_EXPERTS), D_MODEL ** -0.5)
    router_b = nrm((DEPTH, N_EXPERTS), 0.01)
    moe_w1 = nrm((DEPTH, N_EXPERTS, D_MODEL, 2 * EXPERT_FF), D_MODEL ** -0.5)
    moe_b1 = nrm((DEPTH, N_EXPERTS, 2 * EXPERT_FF), 0.01)
    moe_w2 = nrm((DEPTH, N_EXPERTS, EXPERT_FF, D_MODEL), EXPERT_FF ** -0.5)
    moe_b2 = nrm((DEPTH, N_EXPERTS, D_MODEL), 0.01)
    final_norm_g = 1.0 + nrm((D_MODEL,), 0.01)
    return {'x': x, 'c': c, 'ctx': ctx, 'c_ctx': c_ctx, 'mod_w_a': mod_w_a, 'mod_w_b': mod_w_b,
            'mod_b': mod_b, 'norm_mix_g': norm_mix_g, 'norm_ffn_g': norm_ffn_g, 'w_in': w_in,
            'conv_w': conv_w, 'conv_b': conv_b, 'lru_w_a': lru_w_a, 'lru_b_a': lru_b_a,
            'lru_w_x': lru_w_x, 'lru_b_x': lru_b_x, 'lru_lam': lru_lam, 'ret_decay': ret_decay,
            's5_a_re': s5_a_re, 's5_a_im': s5_a_im, 's5_log_dt': s5_log_dt, 's5_b_re': s5_b_re,
            's5_b_im': s5_b_im, 's5_c_re': s5_c_re, 's5_c_im': s5_c_im, 's5_d': s5_d,
            's5_w_glu': s5_w_glu, 's5_b_glu': s5_b_glu, 'w_branch': w_branch, 'w_gate': w_gate,
            'b_gate': b_gate, 'w_out': w_out, 'router_w': router_w, 'router_b': router_b,
            'moe_w1': moe_w1, 'moe_b1': moe_b1, 'moe_w2': moe_w2, 'moe_b2': moe_b2,
            'final_norm_g': final_norm_g}


def reference(x, c, ctx, c_ctx, mod_w_a, mod_w_b, mod_b, norm_mix_g, norm_ffn_g, w_in, conv_w, conv_b,
              lru_w_a, lru_b_a, lru_w_x, lru_b_x, lru_lam, ret_decay, s5_a_re, s5_a_im, s5_log_dt,
              s5_b_re, s5_b_im, s5_c_re, s5_c_im, s5_d, s5_w_glu, s5_b_glu, w_branch, w_gate, b_gate,
              w_out, router_w, router_b, moe_w1, moe_b1, moe_w2, moe_b2, final_norm_g):
    bsz, n, dm = x.shape
    rows = n // GRID_W
    row = jnp.repeat(jnp.arange(rows, dtype=jnp.int32), GRID_W)
    col = jnp.arange(rows * GRID_W, dtype=jnp.int32) % GRID_W
    sc = jax.nn.silu(c)
    scc = jax.nn.silu(c_ctx)
    zc, zl = ctx, x
    for l in range(DEPTH):
        last = l == DEPTH - 1
        mod_l = ((sc @ mod_w_a[l]) @ mod_w_b[l] + mod_b[l]).reshape(bsz, N_MOD, 1, dm)
        mod_c = ((scc @ mod_w_a[l]) @ mod_w_b[l] + mod_b[l]).reshape(N_MOD, 1, 1, dm)
        hc = rms_norm(zc, norm_mix_g[l]) * (1.0 + mod_c[1]) + mod_c[0]
        hl = rms_norm(zl, norm_mix_g[l]) * (1.0 + mod_l[:, 1]) + mod_l[:, 0]
        mc, ml = hybrid_mixer(hc, hl, row, col, w_in[l], conv_w[l], conv_b[l], lru_w_a[l], lru_b_a[l],
                              lru_w_x[l], lru_b_x[l], lru_lam[l], ret_decay[l], s5_a_re[l], s5_a_im[l],
                              s5_log_dt[l], s5_b_re[l], s5_b_im[l], s5_c_re[l], s5_c_im[l], s5_d[l],
                              s5_w_glu[l], s5_b_glu[l], w_branch[l], w_gate[l], b_gate[l], w_out[l],
                              not last)
        zl = zl + mod_l[:, 2] * ml
        hl = rms_norm(zl, norm_ffn_g[l]) * (1.0 + mod_l[:, 4]) + mod_l[:, 3]
        zl = zl + mod_l[:, 5] * moe_ffn(hl, router_w[l], router_b[l], moe_w1[l], moe_b1[l], moe_w2[l], moe_b2[l])
        if not last:
            zc = zc + mod_c[2] * mc
            hc = rms_norm(zc, norm_ffn_g[l]) * (1.0 + mod_c[4]) + mod_c[3]
            zc = zc + mod_c[5] * moe_ffn(hc, router_w[l], router_b[l], moe_w1[l], moe_b1[l], moe_w2[l], moe_b2[l])
    return rms_norm(zl, final_norm_g)
```

```python
import functools
import math

import jax
import jax.numpy as jnp
from jax import lax
from jax.experimental import pallas as pl
from jax.experimental.pallas import tpu as pltpu

F32 = jnp.float32
BF16 = jnp.bfloat16
HIGHEST = lax.Precision.HIGHEST

V7X_VMEM_BYTES = 64 * 1024 * 1024
VMEM_LIMIT = V7X_VMEM_BYTES - 8 * 1024 * 1024
SUBLANES = 8
LANES = 128

N_IN_SPLITS = 7
N_BRANCH = 3
N_MOD = 6
LRU_BLOCKS = 8
LRU_C = 8.0
CONV_W = 4
RET_HEAD_DIM = 128
RET_CHUNK = 128
ROPE_BASE = 10000.0
GRID_W = 64
S5_IN = 16
S5_STATE = 64
TOP_K = 4
SWIGLU_LIMIT = 7.0
SWIGLU_ALPHA = 1.702
NORM_EPS = 1e-6


def _cparams(*sem):
    return pltpu.CompilerParams(dimension_semantics=sem, vmem_limit_bytes=VMEM_LIMIT)


def _row_tile(t, target):
    best = None
    for d in range(16, min(t, target) + 1, 16):
        if t % d == 0:
            best = d
    assert best is not None, (t, target)
    return best


def _col_tile(n, target):
    best = None
    for d in range(LANES, min(n, target) + 1, LANES):
        if n % d == 0:
            best = d
    assert best is not None, (n, target)
    return best


def _softplus(x):
    return jnp.maximum(x, 0.0) + jnp.log1p(jnp.exp(-jnp.abs(x)))


def _is_ctx_rows(i, tm, t_len, c_len):
    row = (i * tm) % t_len + lax.broadcasted_iota(jnp.int32, (tm, 1), 0)
    return row < c_len


def _mod_kernel(cc_ref, wa_ref, wb_ref, b_ref, o_ref):
    cc = cc_ref[...]
    s = cc * jax.nn.sigmoid(cc)
    t = jnp.dot(s, wa_ref[...], precision=HIGHEST, preferred_element_type=F32)
    o_ref[...] = jnp.dot(t, wb_ref[...], precision=HIGHEST, preferred_element_type=F32) + b_ref[...]


def _modulation(cc, mod_w_a, mod_w_b, mod_b):
    depth, d, r = mod_w_a.shape
    rows = cc.shape[0]
    return pl.pallas_call(
        _mod_kernel,
        out_shape=jax.ShapeDtypeStruct((depth, rows, N_MOD * d), F32),
        grid=(depth, N_MOD),
        in_specs=[
            pl.BlockSpec((rows, d), lambda l, j: (0, 0)),
            pl.BlockSpec((None, d, r), lambda l, j: (l, 0, 0)),
            pl.BlockSpec((None, r, d), lambda l, j: (l, 0, j)),
            pl.BlockSpec((None, 1, d), lambda l, j: (l, 0, j)),
        ],
        out_specs=pl.BlockSpec((None, rows, d), lambda l, j: (l, 0, j)),
        compiler_params=_cparams("arbitrary", "arbitrary"),
        name="modulation",
    )(cc, mod_w_a, mod_w_b, mod_b.reshape(depth, 1, N_MOD * d))


def _normed(z_ref, g_ref):
    x = z_ref[...]
    return x * lax.rsqrt(jnp.mean(x * x, axis=-1, keepdims=True) + NORM_EPS) * g_ref[...]


def _norm_mod_kernel(z_ref, g_ref, scl_ref, shl_ref, scc_ref, shc_ref, o_ref, *, tm, t_len, c_len):
    y = _normed(z_ref, g_ref)
    is_ctx = _is_ctx_rows(pl.program_id(0), tm, t_len, c_len)
    scale = jnp.where(is_ctx, scc_ref[...], scl_ref[...])
    shift = jnp.where(is_ctx, shc_ref[...], shl_ref[...])
    o_ref[...] = (y * (1.0 + scale) + shift).astype(o_ref.dtype)


def _norm_router_kernel(z_ref, g_ref, scl_ref, shl_ref, scc_ref, shc_ref, rw_ref, rb_ref,
                        o_ref, comb_ref, *, tm, t_len, c_len):
    y = _normed(z_ref, g_ref)
    is_ctx = _is_ctx_rows(pl.program_id(0), tm, t_len, c_len)
    scale = jnp.where(is_ctx, scc_ref[...], scl_ref[...])
    shift = jnp.where(is_ctx, shc_ref[...], shl_ref[...])
    h = y * (1.0 + scale) + shift
    o_ref[...] = h.astype(o_ref.dtype)
    logits = jnp.dot(h, rw_ref[...], precision=HIGHEST, preferred_element_type=F32) + rb_ref[...]
    n_exp = logits.shape[1]
    lane = lax.broadcasted_iota(jnp.int32, logits.shape, 1).astype(F32)
    work = logits
    picked = jnp.zeros(logits.shape, F32)
    top = None
    for k in range(TOP_K):
        m = jnp.max(work, axis=-1, keepdims=True)
        if k == 0:
            top = m
        first = jnp.min(jnp.where(work == m, lane, float(n_exp)), axis=-1, keepdims=True)
        hit = lane == first
        picked = jnp.where(hit, 1.0, picked)
        work = jnp.where(hit, -jnp.inf, work)
    ex = jnp.where(picked > 0.0, jnp.exp(logits - top), 0.0)
    comb_ref[...] = ex / jnp.sum(ex, axis=-1, keepdims=True)


def _mod_specs(tm, t_len, n_batch, d, k_scale, k_shift):
    def lat(k):
        return pl.BlockSpec((None, None, 1, d), lambda i, *_: ((i * tm) // t_len, k, 0, 0))

    def ctx(k):
        return pl.BlockSpec((None, None, 1, d), lambda i, *_: (n_batch, k, 0, 0))
    return [lat(k_scale), lat(k_shift), ctx(k_scale), ctx(k_shift)]


def _norm_mod(z, g, mod, k_shift, k_scale, dims, router=None):
    n_batch, t_len, c_len = dims
    n, d = z.shape
    tm = _row_tile(t_len, 272)
    row_spec = pl.BlockSpec((tm, d), lambda i: (i, 0))
    in_specs = [row_spec, pl.BlockSpec((1, d), lambda i: (0, 0))] + _mod_specs(tm, t_len, n_batch, d, k_scale, k_shift)
    args = [z, g.reshape(1, d), mod, mod, mod, mod]
    kw = dict(tm=tm, t_len=t_len, c_len=c_len)
    if router is None:
        return pl.pallas_call(
            functools.partial(_norm_mod_kernel, **kw),
            out_shape=jax.ShapeDtypeStruct((n, d), BF16),
            grid=(n // tm,), in_specs=in_specs, out_specs=row_spec,
            compiler_params=_cparams("parallel"), name="norm_mod",
        )(*args)
    rw, rb = router
    n_exp = rw.shape[1]
    in_specs += [pl.BlockSpec((d, n_exp), lambda i: (0, 0)), pl.BlockSpec((1, n_exp), lambda i: (0, 0))]
    return pl.pallas_call(
        functools.partial(_norm_router_kernel, **kw),
        out_shape=(jax.ShapeDtypeStruct((n, d), BF16), jax.ShapeDtypeStruct((n, n_exp), F32)),
        grid=(n // tm,), in_specs=in_specs,
        out_specs=(row_spec, pl.BlockSpec((tm, n_exp), lambda i: (i, 0))),
        compiler_params=_cparams("parallel"), name="norm_router",
    )(*args, rw, rb.reshape(1, n_exp))


def _final_norm_kernel(z_ref, g_ref, o_ref):
    o_ref[...] = _normed(z_ref, g_ref)


def _final_norm(z, g, dims):
    n_batch, t_len, c_len = dims
    n, d = z.shape
    l_len = t_len - c_len
    tm = math.gcd(c_len, l_len)
    per_b = l_len // tm
    off = c_len // tm
    return pl.pallas_call(
        _final_norm_kernel,
        out_shape=jax.ShapeDtypeStruct((n_batch * l_len, d), F32),
        grid=(n_batch, per_b),
        in_specs=[pl.BlockSpec((tm, d), lambda b, s: (b * (t_len // tm) + off + s, 0)),
                  pl.BlockSpec((1, d), lambda b, s: (0, 0))],
        out_specs=pl.BlockSpec((tm, d), lambda b, s: (b * per_b + s, 0)),
        compiler_params=_cparams("parallel", "parallel"), name="final_norm",
    )(z, g.reshape(1, d)).reshape(n_batch, l_len, d)


def _mm_kernel(x_ref, w_ref, o_ref):
    o_ref[...] = jnp.dot(x_ref[...], w_ref[...], preferred_element_type=F32).astype(o_ref.dtype)


def _matmul(x, w, out_dtype, tm_target=1088, tn_target=512):
    n, k = x.shape
    n_out = w.shape[1]
    tm = _row_tile(n, tm_target)
    tn = _col_tile(n_out, tn_target)
    return pl.pallas_call(
        _mm_kernel,
        out_shape=jax.ShapeDtypeStruct((n, n_out), out_dtype),
        grid=(n // tm, n_out // tn),
        in_specs=[pl.BlockSpec((tm, k), lambda i, j: (i, 0)),
                  pl.BlockSpec((k, tn), lambda i, j: (0, j))],
        out_specs=pl.BlockSpec((tm, tn), lambda i, j: (i, j)),
        compiler_params=_cparams("parallel", "arbitrary"), name="matmul",
    )(x, w)


def _mm_residual_kernel(x_ref, w_ref, z_ref, gl_ref, gc_ref, o_ref, *, tm, t_len, c_len):
    acc = jnp.dot(x_ref[...], w_ref[...], preferred_element_type=F32)
    gate = jnp.where(_is_ctx_rows(pl.program_id(0), tm, t_len, c_len), gc_ref[...], gl_ref[...])
    o_ref[...] = z_ref[...] + gate * acc


def _matmul_residual(x, w, z, mod, k_gate, dims, tm_target=1088, tn_target=512):
    n_batch, t_len, c_len = dims
    n, k = x.shape
    d = w.shape[1]
    tm = _row_tile(t_len, tm_target)
    tn = _col_tile(d, tn_target)
    return pl.pallas_call(
        functools.partial(_mm_residual_kernel, tm=tm, t_len=t_len, c_len=c_len),
        out_shape=jax.ShapeDtypeStruct((n, d), F32),
        grid=(n // tm, d // tn),
        in_specs=[pl.BlockSpec((tm, k), lambda i, j: (i, 0)),
                  pl.BlockSpec((k, tn), lambda i, j: (0, j)),
                  pl.BlockSpec((tm, tn), lambda i, j: (i, j)),
                  pl.BlockSpec((None, None, 1, tn), lambda i, j: ((i * tm) // t_len, k_gate, 0, j)),
                  pl.BlockSpec((None, None, 1, tn), lambda i, j: (n_batch, k_gate, 0, j))],
        out_specs=pl.BlockSpec((tm, tn), lambda i, j: (i, j)),
        input_output_aliases={2: 0},
        compiler_params=_cparams("parallel", "arbitrary"), name="matmul_residual",
    )(x, w, z, mod, mod)


def _glu_kernel(x_ref, w_ref, b_ref, y_ref, o_ref):
    acc = jnp.dot(x_ref[...], w_ref[...], preferred_element_type=F32) + b_ref[...]
    o_ref[...] = (y_ref[...].astype(F32) * jax.nn.sigmoid(acc)).astype(o_ref.dtype)


def _glu(y, w, b):
    n, k = y.shape
    tm = _row_tile(n, 1088)
    tn = _col_tile(k, 512)
    return pl.pallas_call(
        _glu_kernel,
        out_shape=jax.ShapeDtypeStruct((n, k), BF16),
        grid=(n // tm, k // tn),
        in_specs=[pl.BlockSpec((tm, k), lambda i, j: (i, 0)),
                  pl.BlockSpec((k, tn), lambda i, j: (0, j)),
                  pl.BlockSpec((1, tn), lambda i, j: (0, j)),
                  pl.BlockSpec((tm, tn), lambda i, j: (i, j))],
        out_specs=pl.BlockSpec((tm, tn), lambda i, j: (i, j)),
        compiler_params=_cparams("parallel", "arbitrary"), name="s5_glu",
    )(y, w, b.reshape(1, k), y)


def _merge_kernel(h_ref, ya_ref, yb_ref, yc_ref, wg_ref, bg_ref, wb_ref, o_ref):
    h = h_ref[...]
    acc = None
    for b, y_ref in enumerate((ya_ref, yb_ref, yc_ref)):
        gate = jax.nn.sigmoid(jnp.dot(h, wg_ref[b], preferred_element_type=F32) + bg_ref[b])
        term = gate * jnp.dot(y_ref[...], wb_ref[b], preferred_element_type=F32)
        acc = term if acc is None else acc + term
    o_ref[...] = acc.astype(o_ref.dtype)


def _merge(h, ya, yb, yc, w_gate, b_gate, w_branch, tm_target=544, tn_target=256):
    n, d = h.shape
    w = ya.shape[1]
    tm = _row_tile(n, tm_target)
    tn = _col_tile(d, tn_target)
    y_spec = pl.BlockSpec((tm, w), lambda i, j: (i, 0))
    return pl.pallas_call(
        _merge_kernel,
        out_shape=jax.ShapeDtypeStruct((n, d), BF16),
        grid=(n // tm, d // tn),
        in_specs=[pl.BlockSpec((tm, d), lambda i, j: (i, 0)), y_spec, y_spec, y_spec,
                  pl.BlockSpec((N_BRANCH, d, tn), lambda i, j: (0, 0, j)),
                  pl.BlockSpec((N_BRANCH, 1, tn), lambda i, j: (0, 0, j)),
                  pl.BlockSpec((N_BRANCH, w, tn), lambda i, j: (0, 0, j))],
        out_specs=pl.BlockSpec((tm, tn), lambda i, j: (i, j)),
        compiler_params=_cparams("parallel", "arbitrary"), name="merge",
    )(h, ya, yb, yc, w_gate, b_gate.reshape(N_BRANCH, 1, d), w_branch)


def _static_chunks(start, size, step):
    return [(s, min(step, start + size - s)) for s in range(start, start + size, step)]


def _scan_block_real(a, b, row, reverse):
    for s in (1, 2, 4):
        if reverse:
            keep = row < SUBLANES - s
            shift = SUBLANES - s
        else:
            keep = row >= s
            shift = s
        a_prev = jnp.where(keep, pltpu.roll(a, shift, 0), 1.0)
        b_prev = jnp.where(keep, pltpu.roll(b, shift, 0), 0.0)
        b = a * b_prev + b
        a = a * a_prev
    return a, b


def _lru_kernel(x_ref, g_ref, cw_ref, cb_ref, wa_ref, ba_ref, wx_ref, bx_ref, lam_ref, o_ref,
                xp_s, xc_s, a_s, b_s, y_s, *, c_len, l_len, rb):
    t_len = c_len + l_len
    width = x_ref.shape[1]
    cw = cw_ref[...]
    cb = cb_ref[...]
    zeros8 = jnp.zeros((SUBLANES, width), F32)
    for seg0, seg_n in ((0, c_len), (c_len, l_len)):
        xp_s[0:SUBLANES, :] = zeros8
        xp_s[SUBLANES + seg_n:2 * SUBLANES + seg_n, :] = zeros8
        for s0, sn in _static_chunks(0, seg_n, rb):
            xp_s[SUBLANES + s0:SUBLANES + s0 + sn, :] = x_ref[seg0 + s0:seg0 + s0 + sn, :]
        for s0, sn in _static_chunks(0, seg_n, rb):
            acc = cb + xp_s[SUBLANES - 1 + s0:SUBLANES - 1 + s0 + sn, :] * cw[0:1]
            for j in range(1, CONV_W):
                acc = acc + xp_s[SUBLANES - 1 + j + s0:SUBLANES - 1 + j + s0 + sn, :] * cw[j:j + 1]
            xc_s[seg0 + s0:seg0 + s0 + sn, :] = acc

    row = lax.broadcasted_iota(jnp.int32, (SUBLANES, width), 0)
    n_blk = t_len // SUBLANES
    c_blk = c_len // SUBLANES

    for d in range(2):
        wa = wa_ref[d]
        wx = wx_ref[d]
        ba = ba_ref[d:d + 1, :]
        bx = bx_ref[d:d + 1, :]
        sp = _softplus(-lam_ref[d:d + 1, :])

        def gates(r, carry, wa=wa, wx=wx, ba=ba, bx=bx, sp=sp):
            rows = pl.ds(pl.multiple_of(r * rb, rb), rb)
            xc = xc_s[rows, :]
            xb = xc.astype(BF16)
            rr = jax.nn.sigmoid(jnp.dot(xb, wa, preferred_element_type=F32) + ba)
            ii = jax.nn.sigmoid(jnp.dot(xb, wx, preferred_element_type=F32) + bx)
            log_a = (-LRU_C) * rr * sp
            th = jnp.tanh(log_a)
            a_s[rows, :] = jnp.exp(log_a)
            b_s[rows, :] = jnp.sqrt(-2.0 * th / (1.0 - th)) * (ii * xc)
            return carry
        lax.fori_loop(0, t_len // rb, gates, 0)

        reverse = d == 1

        def scan(i, h, reverse=reverse, first=(d == 0), lo=0, hi=n_blk):
            blk = (hi - 1 - i) if reverse else (lo + i)
            rows = pl.ds(pl.multiple_of(blk * SUBLANES, SUBLANES), SUBLANES)
            a_cum, b_cum = _scan_block_real(a_s[rows, :], b_s[rows, :], row, reverse)
            hs = a_cum * h + b_cum
            if first:
                y_s[rows, :] = hs
            else:
                y_s[rows, :] = y_s[rows, :] + hs
            return hs[0:1, :] if reverse else hs[SUBLANES - 1:SUBLANES, :]

        h0 = jnp.zeros((1, width), F32)
        if not reverse:
            lax.fori_loop(0, n_blk, scan, h0)
        else:
            h_ctx = lax.fori_loop(0, c_blk, functools.partial(scan, lo=0, hi=c_blk), h0)
            lax.fori_loop(0, n_blk - c_blk, functools.partial(scan, lo=c_blk, hi=n_blk), h_ctx)

    def finish(r, carry):
        rows = pl.ds(pl.multiple_of(r * rb, rb), rb)
        o_ref[rows, :] = (jax.nn.gelu(g_ref[rows, :]) * y_s[rows, :]).astype(o_ref.dtype)
        return carry
    lax.fori_loop(0, t_len // rb, finish, 0)


def _lru_branch(u, conv_w, conv_b, w_a, b_a, w_x, b_x, lam, dims):
    n_batch, t_len, c_len = dims
    l_len = t_len - c_len
    n = u.shape[0]
    width = conv_w.shape[1]
    bw = width // LRU_BLOCKS
    rb = math.gcd(math.gcd(c_len, l_len), 256)
    vec2 = pl.BlockSpec((2, bw), lambda b, k: (0, k))
    wspec = pl.BlockSpec((2, None, bw, bw), lambda b, k: (0, k, 0, 0))
    return pl.pallas_call(
        functools.partial(_lru_kernel, c_len=c_len, l_len=l_len, rb=rb),
        out_shape=jax.ShapeDtypeStruct((n, width), BF16),
        grid=(n_batch, LRU_BLOCKS),
        in_specs=[pl.BlockSpec((t_len, bw), lambda b, k: (b, k)),
                  pl.BlockSpec((t_len, bw), lambda b, k: (b, LRU_BLOCKS + k)),
                  pl.BlockSpec((CONV_W, bw), lambda b, k: (0, k)),
                  pl.BlockSpec((1, bw), lambda b, k: (0, k)),
                  wspec, vec2, wspec, vec2, vec2],
        out_specs=pl.BlockSpec((t_len, bw), lambda b, k: (b, k)),
        scratch_shapes=[pltpu.VMEM((l_len + 2 * SUBLANES, bw), F32),
                        pltpu.VMEM((t_len, bw), F32), pltpu.VMEM((t_len, bw), F32),
                        pltpu.VMEM((t_len, bw), F32), pltpu.VMEM((t_len, bw), F32)],
        compiler_params=_cparams("parallel", "parallel"), name="rglru",
    )(u, u, conv_w, conv_b.reshape(1, width), w_a.astype(BF16), b_a, w_x.astype(BF16), b_x, lam)


def _ret_kernel(q_ref, k_ref, v_ref, g_ref, cos_ref, sin_ref, dec_ref, o_ref,
                qs_s, ks_s, sb_s, dm_s, vec_s, *, c_len, l_len, rb):
    t_len = c_len + l_len
    hd = q_ref.shape[1]
    ch = RET_CHUNK
    k_scale = hd ** -0.5
    dec = dec_ref[...]
    lg_f = -_softplus(-dec[0])
    lg_b = -_softplus(-dec[1])
    ri = lax.broadcasted_iota(jnp.int32, (ch, ch), 0).astype(F32)
    ci = lax.broadcasted_iota(jnp.int32, (ch, ch), 1).astype(F32)
    diff = ri - ci
    dm_s[...] = (jnp.where(diff >= 0, jnp.exp(lg_f * jnp.maximum(diff, 0.0)), 0.0)
                 + jnp.where(diff <= 0, jnp.exp(lg_b * jnp.maximum(-diff, 0.0)), 0.0))
    pos = lax.broadcasted_iota(jnp.int32, (ch, hd), 0).astype(F32)
    vec_s[0] = jnp.exp(lg_f * (ch - 1.0 - pos))
    vec_s[1] = jnp.exp(lg_f * (pos + 1.0))
    vec_s[2] = jnp.exp(lg_b * pos)
    vec_s[3] = jnp.exp(lg_b * (ch - pos))
    cd_f = jnp.exp(lg_f * float(ch))
    cd_b = jnp.exp(lg_b * float(ch))

    for s0, sn in _static_chunks(0, c_len, rb):
        qs_s[s0:s0 + sn, :] = q_ref[s0:s0 + sn, :]
        ks_s[s0:s0 + sn, :] = k_ref[s0:s0 + sn, :] * k_scale
    lane = lax.broadcasted_iota(jnp.int32, (rb, hd), 1)
    low = (lane % (hd // 2)) < (hd // 4)

    def rope(r, carry):
        src = pl.ds(pl.multiple_of(c_len + r * rb, rb), rb)
        tab = pl.ds(pl.multiple_of(r * rb, rb), rb)
        cs = cos_ref[tab, :]
        sn = sin_ref[tab, :]
        for ref, dst, scale in ((q_ref, qs_s, 1.0), (k_ref, ks_s, k_scale)):
            x = ref[src, :]
            partner = jnp.where(low, pltpu.roll(x, hd - hd // 4, 1), pltpu.roll(x, hd // 4, 1))
            y = x * cs + partner * sn
            dst[src, :] = y * scale if scale != 1.0 else y
        return carry
    lax.fori_loop(0, l_len // rb, rope, 0)

    def chunk_rows(c):
        return pl.ds(pl.multiple_of(c * ch, ch), ch)

    def kv_state(rows, kdec):
        kd = (ks_s[rows, :] * kdec).T.astype(BF16)
        return jnp.dot(kd, v_ref[rows, :].astype(BF16), preferred_element_type=F32)

    n_ch = t_len // ch
    c_ch = c_len // ch

    def back(i, s, lo, hi):
        c = hi - 1 - i
        sb_s[c] = s
        return cd_b * s + kv_state(chunk_rows(c), vec_s[2])
    s0 = jnp.zeros((hd, hd), F32)
    s_ctx = lax.fori_loop(0, c_ch, functools.partial(back, lo=0, hi=c_ch), s0)
    lax.fori_loop(0, n_ch - c_ch, functools.partial(back, lo=c_ch, hi=n_ch), s_ctx)

    def fwd(c, s):
        rows = chunk_rows(c)
        q = qs_s[rows, :]
        qb = q.astype(BF16)
        kb = ks_s[rows, :].astype(BF16)
        vb = v_ref[rows, :].astype(BF16)
        scores = lax.dot_general(qb, kb, (((1,), (1,)), ((), ())), preferred_element_type=F32) * dm_s[...]
        o = jnp.dot(scores.astype(BF16), vb, preferred_element_type=F32)
        o = o + jnp.dot((q * vec_s[1]).astype(BF16), s.astype(BF16), preferred_element_type=F32)
        o = o + jnp.dot((q * vec_s[3]).astype(BF16), sb_s[c].astype(BF16), preferred_element_type=F32)
        o = o * lax.rsqrt(jnp.mean(o * o, axis=-1, keepdims=True) + NORM_EPS)
        g = g_ref[rows, :]
        o_ref[rows, :] = (o * (g * jax.nn.sigmoid(g))).astype(o_ref.dtype)
        return cd_f * s + kv_state(rows, vec_s[0])
    lax.fori_loop(0, n_ch, fwd, s0)


def _rope_tables(l_len, hd):
    t = jnp.arange(l_len, dtype=jnp.int32)
    row = (t // GRID_W).astype(F32)
    col = (t % GRID_W).astype(F32)
    quarter = hd // 4
    freqs = ROPE_BASE ** (-jnp.arange(quarter, dtype=F32) / quarter)
    ang_r = row[:, None] * freqs
    ang_c = col[:, None] * freqs
    cos = jnp.concatenate([jnp.cos(ang_r)] * 2 + [jnp.cos(ang_c)] * 2, axis=-1)
    sin = jnp.concatenate([-jnp.sin(ang_r), jnp.sin(ang_r), -jnp.sin(ang_c), jnp.sin(ang_c)], axis=-1)
    return cos, sin


def _ret_branch(u, cos, sin, decay, dims):
    n_batch, t_len, c_len = dims
    l_len = t_len - c_len
    n = u.shape[0]
    hd = RET_HEAD_DIM
    n_heads = decay.shape[1]
    rb = math.gcd(math.gcd(c_len, l_len), 256)

    def col(first):
        return pl.BlockSpec((t_len, hd), lambda b, h: (b, first + h))
    tab = pl.BlockSpec((l_len, hd), lambda b, h: (0, 0))
    return pl.pallas_call(
        functools.partial(_ret_kernel, c_len=c_len, l_len=l_len, rb=rb),
        out_shape=jax.ShapeDtypeStruct((n, n_heads * hd), BF16),
        grid=(n_batch, n_heads),
        in_specs=[col(2 * n_heads), col(3 * n_heads), col(4 * n_heads), col(5 * n_heads), tab, tab,
                  pl.BlockSpec((2, None, 1, 1), lambda b, h: (0, h, 0, 0))],
        out_specs=pl.BlockSpec((t_len, hd), lambda b, h: (b, h)),
        scratch_shapes=[pltpu.VMEM((t_len, hd), F32), pltpu.VMEM((t_len, hd), F32),
                        pltpu.VMEM((t_len // RET_CHUNK, hd, hd), F32),
                        pltpu.VMEM((RET_CHUNK, RET_CHUNK), F32),
                        pltpu.VMEM((4, RET_CHUNK, hd), F32)],
        compiler_params=_cparams("parallel", "parallel"), name="retention",
    )(u, u, u, u, cos, sin, decay.reshape(2, n_heads, 1, 1))


def _s5_disc_kernel(are_ref, aim_ref, ldt_ref, bre_ref, bim_ref, pre_ref, pim_ref, bbre_ref, bbim_ref):
    lam_re = jnp.minimum(are_ref[...], -1e-4)
    lam_im = aim_ref[...]
    dt = jnp.exp(ldt_ref[...])
    z_re = lam_re * dt
    z_im = lam_im * dt
    mag = jnp.exp(z_re)
    ab_re = mag * jnp.cos(z_im)
    ab_im = mag * jnp.sin(z_im)
    den = lam_re * lam_re + lam_im * lam_im
    n_re = ab_re - 1.0
    co_re = (n_re * lam_re + ab_im * lam_im) / den
    co_im = (ab_im * lam_re - n_re * lam_im) / den
    for i in range(bre_ref.shape[0]):
        b_re = bre_ref[i]
        b_im = bim_ref[i]
        bbre_ref[i] = co_re * b_re - co_im * b_im
        bbim_ref[i] = co_re * b_im + co_im * b_re
    p_re, p_im = ab_re, ab_im
    pre_ref[0] = p_re
    pim_ref[0] = p_im
    for j in range(1, SUBLANES):
        p_re, p_im = p_re * ab_re - p_im * ab_im, p_re * ab_im + p_im * ab_re
        pre_ref[j] = p_re
        pim_ref[j] = p_im


def _s5_discretise(a_re, a_im, log_dt, b_re, b_im):
    two, g, p = a_re.shape
    i = b_re.shape[-1]
    rows = two * g
    full2 = pl.BlockSpec((rows, p), lambda: (0, 0))
    full3 = pl.BlockSpec((i, rows, p), lambda: (0, 0, 0))
    pw = pl.BlockSpec((SUBLANES, rows, p), lambda: (0, 0, 0))

    def input_major(m):
        return jnp.transpose(m, (3, 0, 1, 2)).reshape(i, rows, p)
    pw_re, pw_im, bb_re, bb_im = pl.pallas_call(
        _s5_disc_kernel,
        out_shape=(jax.ShapeDtypeStruct((SUBLANES, rows, p), F32), jax.ShapeDtypeStruct((SUBLANES, rows, p), F32),
                   jax.ShapeDtypeStruct((i, rows, p), F32), jax.ShapeDtypeStruct((i, rows, p), F32)),
        in_specs=[full2, full2, pl.BlockSpec((rows, 1), lambda: (0, 0)), full3, full3],
        out_specs=(pw, pw, full3, full3),
        name="s5_discretise",
    )(a_re.reshape(rows, p), a_im.reshape(rows, p), log_dt.reshape(rows, 1), input_major(b_re), input_major(b_im))
    return pw_re, pw_im, jnp.swapaxes(bb_re, 0, 1), jnp.swapaxes(bb_im, 0, 1)


def _s5_kernel(u_ref, bb_ref, cb_ref, pw_ref, dsk_ref, o_ref, hr_s, hi_s, y_s, lv_s, *, c_len, l_len, rb):
    t_len = c_len + l_len
    ns = hr_s.shape[1]
    row = lax.broadcasted_iota(jnp.int32, (SUBLANES, ns), 0)
    n_blk = t_len // SUBLANES
    c_blk = c_len // SUBLANES
    dsk = dsk_ref[...]

    def skip(r, carry):
        rows = pl.ds(pl.multiple_of(r * rb, rb), rb)
        y_s[rows, :] = dsk * u_ref[rows, :]
        return carry
    lax.fori_loop(0, t_len // rb, skip, 0)

    for d in range(2):
        reverse = d == 1
        bb = bb_ref[d]

        def drive(r, carry, bb=bb):
            rows = pl.ds(pl.multiple_of(r * rb, rb), rb)
            bu = jnp.dot(u_ref[rows, :].astype(BF16), bb, preferred_element_type=F32)
            hr_s[rows, :] = bu[:, :ns]
            hi_s[rows, :] = bu[:, ns:]
            return carry
        lax.fori_loop(0, t_len // rb, drive, 0)

        for lvl, s in enumerate((1, 2, 4)):
            keep = (row < SUBLANES - s) if reverse else (row >= s)
            for part in range(2):
                lv_s[lvl, part] = jnp.where(keep, pw_ref[d, part, s - 1:s, :], 0.0)
        for part in range(2):
            if reverse:
                for j in range(SUBLANES):
                    lv_s[3, part, j:j + 1, :] = pw_ref[d, part, SUBLANES - 1 - j:SUBLANES - j, :]
            else:
                lv_s[3, part] = pw_ref[d, part]

        def scan(i, carry, reverse=reverse, lo=0, hi=n_blk):
            cr, ci = carry
            blk = (hi - 1 - i) if reverse else (lo + i)
            rows = pl.ds(pl.multiple_of(blk * SUBLANES, SUBLANES), SUBLANES)
            hr = hr_s[rows, :]
            hi_ = hi_s[rows, :]
            for lvl, s in enumerate((1, 2, 4)):
                shift = SUBLANES - s if reverse else s
                ar = lv_s[lvl, 0]
                ai = lv_s[lvl, 1]
                sr = pltpu.roll(hr, shift, 0)
                si = pltpu.roll(hi_, shift, 0)
                hr, hi_ = hr + (ar * sr - ai * si), hi_ + (ar * si + ai * sr)
            pr = lv_s[3, 0]
            pi = lv_s[3, 1]
            hr, hi_ = hr + (pr * cr - pi * ci), hi_ + (pr * ci + pi * cr)
            hr_s[rows, :] = hr
            hi_s[rows, :] = hi_
            if reverse:
                return hr[0:1, :], hi_[0:1, :]
            return hr[SUBLANES - 1:SUBLANES, :], hi_[SUBLANES - 1:SUBLANES, :]

        zero = (jnp.zeros((1, ns), F32), jnp.zeros((1, ns), F32))
        if not reverse:
            lax.fori_loop(0, n_blk, scan, zero)
        else:
            mid = lax.fori_loop(0, c_blk, functools.partial(scan, lo=0, hi=c_blk), zero)
            lax.fori_loop(0, n_blk - c_blk, functools.partial(scan, lo=c_blk, hi=n_blk), mid)

        cb = cb_ref[d]

        def readout(r, carry, cb=cb):
            rows = pl.ds(pl.multiple_of(r * rb, rb), rb)
            y = jnp.dot(hr_s[rows, :].astype(BF16), cb[:ns, :], preferred_element_type=F32)
            y = y + jnp.dot(hi_s[rows, :].astype(BF16), cb[ns:, :], preferred_element_type=F32)
            y_s[rows, :] = y_s[rows, :] + y
            return carry
        lax.fori_loop(0, t_len // rb, readout, 0)

    def finish(r, carry):
        rows = pl.ds(pl.multiple_of(r * rb, rb), rb)
        o_ref[rows, :] = jax.nn.gelu(y_s[rows, :]).astype(o_ref.dtype)
        return carry
    lax.fori_loop(0, t_len // rb, finish, 0)


def _s5_branch(u, a_re, a_im, log_dt, b_re, b_im, c_re, c_im, d_skip, dims):
    n_batch, t_len, c_len = dims
    l_len = t_len - c_len
    n = u.shape[0]
    _, g, p = a_re.shape
    i = b_re.shape[-1]
    width = g * i
    gpb = LANES // i
    nb = g // gpb
    ns = gpb * p
    pw_re, pw_im, bb_re, bb_im = _s5_discretise(a_re, a_im, log_dt, b_re, b_im)
    eye = jnp.eye(gpb, dtype=F32)

    def blockdiag_in(m):
        m = m.reshape(2, nb, gpb, i, p)
        return jnp.einsum('dbgip,gh->dbgihp', m, eye).reshape(2, nb, gpb * i, ns)
    bb = jnp.concatenate([blockdiag_in(bb_re), blockdiag_in(bb_im)], axis=-1).astype(BF16)

    def blockdiag_out(m):
        m = m.reshape(2, nb, gpb, i, p)
        return jnp.einsum('dbgip,gh->dbgphi', m, eye).reshape(2, nb, ns, gpb * i)
    cb = jnp.concatenate([blockdiag_out(c_re), -blockdiag_out(c_im)], axis=-2).astype(BF16)
    pw = jnp.stack([pw_re.reshape(SUBLANES, 2, g * p), pw_im.reshape(SUBLANES, 2, g * p)], axis=0)
    pw = jnp.transpose(pw, (2, 0, 1, 3))
    rb = math.gcd(math.gcd(c_len, l_len), 256)
    cols = (N_IN_SPLITS - 1) * width // LANES
    return pl.pallas_call(
        functools.partial(_s5_kernel, c_len=c_len, l_len=l_len, rb=rb),
        out_shape=jax.ShapeDtypeStruct((n, width), BF16),
        grid=(n_batch, nb),
        in_specs=[pl.BlockSpec((t_len, LANES), lambda b, k: (b, cols + k)),
                  pl.BlockSpec((2, None, LANES, 2 * ns), lambda b, k: (0, k, 0, 0)),
                  pl.BlockSpec((2, None, 2 * ns, LANES), lambda b, k: (0, k, 0, 0)),
                  pl.BlockSpec((2, 2, SUBLANES, ns), lambda b, k: (0, 0, 0, k)),
                  pl.BlockSpec((1, LANES), lambda b, k: (0, k))],
        out_specs=pl.BlockSpec((t_len, LANES), lambda b, k: (b, k)),
        scratch_shapes=[pltpu.VMEM((t_len, ns), F32), pltpu.VMEM((t_len, ns), F32),
                        pltpu.VMEM((t_len, LANES), F32),
                        pltpu.VMEM((4, 2, SUBLANES, ns), F32)],
        compiler_params=_cparams("parallel", "parallel"), name="s5",
    )(u, bb, cb, pw, d_skip.reshape(1, width))


def _moe_kernel(h_ref, comb_ref, w1_ref, b1_ref, w2_ref, b2_ref, o_ref, *, ff, dc):
    e = pl.program_id(1)
    hu = jnp.dot(h_ref[...], w1_ref[...], preferred_element_type=F32) + b1_ref[...]
    gate = jnp.minimum(hu[:, :ff], SWIGLU_LIMIT)
    up = jnp.clip(hu[:, ff:], -SWIGLU_LIMIT, SWIGLU_LIMIT)
    act = (gate * jax.nn.sigmoid(SWIGLU_ALPHA * gate) * (up + 1.0)).astype(BF16)
    comb = comb_ref[...]
    lane = lax.broadcasted_iota(jnp.int32, comb.shape, 1)
    weight = jnp.sum(jnp.where(lane == e, comb, 0.0), axis=-1, keepdims=True)
    d = o_ref.shape[1]
    for c0 in range(0, d, dc):
        y = jnp.dot(act, w2_ref[:, c0:c0 + dc], preferred_element_type=F32) + b2_ref[:, c0:c0 + dc]
        contrib = weight * y

        @pl.when(e == 0)
        def _(contrib=contrib, c0=c0):
            o_ref[:, c0:c0 + dc] = contrib

        @pl.when(e != 0)
        def _(contrib=contrib, c0=c0):
            o_ref[:, c0:c0 + dc] = o_ref[:, c0:c0 + dc] + contrib


def _moe(h, comb, w1, b1, w2, b2):
    n, d = h.shape
    n_exp, _, ff2 = w1.shape
    ff = ff2 // 2
    tm = _row_tile(n, 544)
    return pl.pallas_call(
        functools.partial(_moe_kernel, ff=ff, dc=_col_tile(d, 1024)),
        out_shape=jax.ShapeDtypeStruct((n, d), F32),
        grid=(n // tm, n_exp),
        in_specs=[pl.BlockSpec((tm, d), lambda i, e: (i, 0)),
                  pl.BlockSpec((tm, n_exp), lambda i, e: (i, 0)),
                  pl.BlockSpec((None, d, ff2), lambda i, e: (e, 0, 0)),
                  pl.BlockSpec((None, 1, ff2), lambda i, e: (e, 0, 0)),
                  pl.BlockSpec((None, ff, d), lambda i, e: (e, 0, 0)),
                  pl.BlockSpec((None, 1, d), lambda i, e: (e, 0, 0))],
        out_specs=pl.BlockSpec((tm, d), lambda i, e: (i, 0)),
        compiler_params=_cparams("parallel", "arbitrary"), name="moe",
    )(h, comb, w1, b1.reshape(n_exp, 1, ff2), w2, b2.reshape(n_exp, 1, d))


def _residual_kernel(z_ref, y_ref, gl_ref, gc_ref, o_ref, *, tm, t_len, c_len):
    gate = jnp.where(_is_ctx_rows(pl.program_id(0), tm, t_len, c_len), gc_ref[...], gl_ref[...])
    o_ref[...] = z_ref[...] + gate * y_ref[...]


def _residual(z, y, mod, k_gate, dims):
    n_batch, t_len, c_len = dims
    n, d = z.shape
    tm = _row_tile(t_len, 272)
    row_spec = pl.BlockSpec((tm, d), lambda i: (i, 0))
    return pl.pallas_call(
        functools.partial(_residual_kernel, tm=tm, t_len=t_len, c_len=c_len),
        out_shape=jax.ShapeDtypeStruct((n, d), F32),
        grid=(n // tm,),
        in_specs=[row_spec, row_spec,
                  pl.BlockSpec((None, None, 1, d), lambda i: ((i * tm) // t_len, k_gate, 0, 0)),
                  pl.BlockSpec((None, None, 1, d), lambda i: (n_batch, k_gate, 0, 0))],
        out_specs=row_spec,
        input_output_aliases={0: 0},
        compiler_params=_cparams("parallel"), name="residual",
    )(z, y, mod, mod)


def _layer(z, p, cos, sin, dims):
    d = z.shape[1]
    mod = p['mod'].reshape(p['mod'].shape[0], N_MOD, 1, d)
    h = _norm_mod(z, p['norm_mix_g'], mod, 0, 1, dims)
    u = _matmul(h, p['w_in'].astype(BF16), F32)
    ya = _lru_branch(u, p['conv_w'], p['conv_b'], p['lru_w_a'], p['lru_b_a'], p['lru_w_x'], p['lru_b_x'],
                     p['lru_lam'], dims)
    yb = _ret_branch(u, cos, sin, p['ret_decay'], dims)
    yc = _s5_branch(u, p['s5_a_re'], p['s5_a_im'], p['s5_log_dt'], p['s5_b_re'], p['s5_b_im'],
                    p['s5_c_re'], p['s5_c_im'], p['s5_d'], dims)
    yc = _glu(yc, p['s5_w_glu'].astype(BF16), p['s5_b_glu'])
    m = _merge(h, ya, yb, yc, p['w_gate'].astype(BF16), p['b_gate'], p['w_branch'].astype(BF16))
    z = _matmul_residual(m, p['w_out'].astype(BF16), z, mod, 2, dims)
    h, comb = _norm_mod(z, p['norm_ffn_g'], mod, 3, 4, dims, router=(p['router_w'], p['router_b']))
    y = _moe(h, comb, p['moe_w1'].astype(BF16), p['moe_b1'], p['moe_w2'].astype(BF16), p['moe_b2'])
    return _residual(z, y, mod, 5, dims)


def kernel(x, c, ctx, c_ctx, mod_w_a, mod_w_b, mod_b, norm_mix_g, norm_ffn_g, w_in, conv_w, conv_b, lru_w_a, lru_b_a, lru_w_x, lru_b_x, lru_lam, ret_decay, s5_a_re, s5_a_im, s5_log_dt, s5_b_re, s5_b_im, s5_c_re, s5_c_im, s5_d, s5_w_glu, s5_b_glu, w_branch, w_gate, b_gate, w_out, router_w, router_b, moe_w1, moe_b1, moe_w2, moe_b2, final_norm_g):
    n_batch, l_len, d = x.shape
    c_len = ctx.shape[1]
    t_len = c_len + l_len
    dims = (n_batch, t_len, c_len)
    z = jnp.concatenate([ctx, x], axis=1).reshape(n_batch * t_len, d)
    pad = (-(n_batch + 1)) % SUBLANES
    cc = jnp.concatenate([c, c_ctx[None, :], jnp.zeros((pad, d), F32)], axis=0)
    mod = _modulation(cc, mod_w_a, mod_w_b, mod_b)
    cos, sin = _rope_tables(l_len, RET_HEAD_DIM)
    params = dict(mod=mod, norm_mix_g=norm_mix_g, norm_ffn_g=norm_ffn_g, w_in=w_in, conv_w=conv_w, conv_b=conv_b,
                  lru_w_a=lru_w_a, lru_b_a=lru_b_a, lru_w_x=lru_w_x, lru_b_x=lru_b_x, lru_lam=lru_lam,
                  ret_decay=ret_decay, s5_a_re=s5_a_re, s5_a_im=s5_a_im, s5_log_dt=s5_log_dt, s5_b_re=s5_b_re,
                  s5_b_im=s5_b_im, s5_c_re=s5_c_re, s5_c_im=s5_c_im, s5_d=s5_d, s5_w_glu=s5_w_glu,
                  s5_b_glu=s5_b_glu, w_branch=w_branch, w_gate=w_gate, b_gate=b_gate, w_out=w_out,
                  router_w=router_w, router_b=router_b, moe_w1=moe_w1, moe_b1=moe_b1, moe_w2=moe_w2,
                  moe_b2=moe_b2)

    def body(zc, p):
        return _layer(zc, p, cos, sin, dims), None
    z, _ = lax.scan(body, z, params)
    return _final_norm(z, final_norm_g, dims)
```

```python
import functools
import math

import jax
import jax.numpy as jnp
from jax import lax
from jax.experimental import pallas as pl
from jax.experimental.pallas import tpu as pltpu

F32 = jnp.float32
BF16 = jnp.bfloat16
HIGHEST = lax.Precision.HIGHEST

V7X_VMEM_BYTES = 64 * 1024 * 1024
VMEM_LIMIT = V7X_VMEM_BYTES - 8 * 1024 * 1024
SUBLANES = 8
LANES = 128

N_IN_SPLITS = 7
N_BRANCH = 3
N_MOD = 6
LRU_BLOCKS = 8
LRU_C = 8.0
CONV_W = 4
RET_HEAD_DIM = 128
RET_CHUNK = 128
ROPE_BASE = 10000.0
GRID_W = 64
S5_IN = 16
S5_STATE = 64
TOP_K = 4
SWIGLU_LIMIT = 7.0
SWIGLU_ALPHA = 1.702
NORM_EPS = 1e-6


def _cparams(*sem):
    return pltpu.CompilerParams(dimension_semantics=sem, vmem_limit_bytes=VMEM_LIMIT)


def _row_tile(t, target):
    best = None
    for d in range(16, min(t, target) + 1, 16):
        if t % d == 0:
            best = d
    assert best is not None, (t, target)
    return best


def _col_tile(n, target):
    best = None
    for d in range(LANES, min(n, target) + 1, LANES):
        if n % d == 0:
            best = d
    assert best is not None, (n, target)
    return best


def _softplus(x):
    return jnp.maximum(x, 0.0) + jnp.log1p(jnp.exp(-jnp.abs(x)))


def _is_ctx_rows(i, tm, t_len, c_len):
    row = (i * tm) % t_len + lax.broadcasted_iota(jnp.int32, (tm, 1), 0)
    return row < c_len


def _mod_kernel(cc_ref, wa_ref, wb_ref, b_ref, o_ref):
    cc = cc_ref[...]
    s = cc * jax.nn.sigmoid(cc)
    t = jnp.dot(s, wa_ref[...], precision=HIGHEST, preferred_element_type=F32)
    o_ref[...] = jnp.dot(t, wb_ref[...], precision=HIGHEST, preferred_element_type=F32) + b_ref[...]


def _modulation(cc, mod_w_a, mod_w_b, mod_b):
    depth, d, r = mod_w_a.shape
    rows = cc.shape[0]
    return pl.pallas_call(
        _mod_kernel,
        out_shape=jax.ShapeDtypeStruct((depth, rows, N_MOD * d), F32),
        grid=(depth, N_MOD),
        in_specs=[
            pl.BlockSpec((rows, d), lambda l, j: (0, 0)),
            pl.BlockSpec((None, d, r), lambda l, j: (l, 0, 0)),
            pl.BlockSpec((None, r, d), lambda l, j: (l, 0, j)),
            pl.BlockSpec((None, 1, d), lambda l, j: (l, 0, j)),
        ],
        out_specs=pl.BlockSpec((None, rows, d), lambda l, j: (l, 0, j)),
        compiler_params=_cparams("arbitrary", "arbitrary"),
        name="modulation",
    )(cc, mod_w_a, mod_w_b, mod_b.reshape(depth, 1, N_MOD * d))


def _normed(z_ref, g_ref):
    x = z_ref[...]
    return x * lax.rsqrt(jnp.mean(x * x, axis=-1, keepdims=True) + NORM_EPS) * g_ref[...]


def _norm_mod_kernel(z_ref, g_ref, scl_ref, shl_ref, scc_ref, shc_ref, o_ref, *, tm, t_len, c_len):
    y = _normed(z_ref, g_ref)
    is_ctx = _is_ctx_rows(pl.program_id(0), tm, t_len, c_len)
    scale = jnp.where(is_ctx, scc_ref[...], scl_ref[...])
    shift = jnp.where(is_ctx, shc_ref[...], shl_ref[...])
    o_ref[...] = (y * (1.0 + scale) + shift).astype(o_ref.dtype)


HIGH_HALF = 0xFFFF0000


def _pack_halves(lo, hi):
    lo_bits = pltpu.bitcast(lo.astype(BF16).astype(F32), jnp.uint32)
    hi_bits = pltpu.bitcast(hi.astype(BF16).astype(F32), jnp.uint32)
    return (lo_bits >> jnp.uint32(16)) | (hi_bits & jnp.uint32(HIGH_HALF))


def _pack_bf16_pairs(x):
    half = x.shape[1] // 2
    return _pack_halves(x[:, :half], x[:, half:])


def _unpack_bf16_pairs(w):
    return pltpu.bitcast(w << jnp.uint32(16), F32), pltpu.bitcast(w & jnp.uint32(HIGH_HALF), F32)


def _norm_router_kernel(z_ref, g_ref, scl_ref, shl_ref, scc_ref, shc_ref, rw_ref, rb_ref,
                        hp_ref, idx_ref, rank_ref, prob_ref, cnt_ref, cnt_s, *, tm, t_len, c_len):
    i = pl.program_id(0)

    @pl.when(i == 0)
    def _():
        cnt_s[...] = jnp.zeros(cnt_s.shape, F32)
    y = _normed(z_ref, g_ref)
    is_ctx = _is_ctx_rows(i, tm, t_len, c_len)
    scale = jnp.where(is_ctx, scc_ref[...], scl_ref[...])
    shift = jnp.where(is_ctx, shc_ref[...], shl_ref[...])
    h = y * (1.0 + scale) + shift
    hp_ref[...] = _pack_bf16_pairs(h)
    logits = jnp.dot(h, rw_ref[...], precision=HIGHEST, preferred_element_type=F32) + rb_ref[...]
    n_exp = logits.shape[1]
    lane = lax.broadcasted_iota(jnp.int32, logits.shape, 1).astype(F32)
    work = logits
    picked = jnp.zeros(logits.shape, F32)
    firsts, vals = [], []
    for k in range(TOP_K):
        m = jnp.max(work, axis=-1, keepdims=True)
        first = jnp.min(jnp.where(work == m, lane, float(n_exp)), axis=-1, keepdims=True)
        hit = lane == first
        picked = jnp.where(hit, 1.0, picked)
        work = jnp.where(hit, -jnp.inf, work)
        firsts.append(first)
        vals.append(m)
    tri = (lax.broadcasted_iota(jnp.int32, (tm, tm), 0) > lax.broadcasted_iota(jnp.int32, (tm, tm), 1))
    before = jnp.dot(jnp.where(tri, 1.0, 0.0).astype(BF16), picked.astype(BF16), preferred_element_type=F32)
    before = before + cnt_s[...]
    ex = [jnp.exp(v - vals[0]) for v in vals]
    denom = ex[0] + ex[1] + ex[2] + ex[3]
    out_lane = lax.broadcasted_iota(jnp.int32, idx_ref.shape, 1)
    idx_out = jnp.zeros(idx_ref.shape, F32)
    rank_out = jnp.zeros(idx_ref.shape, F32)
    prob_out = jnp.zeros(idx_ref.shape, F32)
    for k in range(TOP_K):
        rank_k = jnp.sum(jnp.where(lane == firsts[k], before, 0.0), axis=-1, keepdims=True)
        idx_out = jnp.where(out_lane == k, firsts[k], idx_out)
        rank_out = jnp.where(out_lane == k, rank_k, rank_out)
        prob_out = jnp.where(out_lane == k, ex[k] / denom, prob_out)
    idx_ref[...] = idx_out.astype(jnp.int32)
    rank_ref[...] = rank_out.astype(jnp.int32)
    prob_ref[...] = prob_out
    cnt_s[...] = cnt_s[...] + jnp.sum(picked, axis=0, keepdims=True)
    cnt_ref[...] = cnt_s[...]


def _mod_specs(tm, t_len, n_batch, d, k_scale, k_shift):
    def lat(k):
        return pl.BlockSpec((None, None, 1, d), lambda i, *_: ((i * tm) // t_len, k, 0, 0))

    def ctx(k):
        return pl.BlockSpec((None, None, 1, d), lambda i, *_: (n_batch, k, 0, 0))
    return [lat(k_scale), lat(k_shift), ctx(k_scale), ctx(k_shift)]


def _norm_mod(z, g, mod, k_shift, k_scale, dims):
    n_batch, t_len, c_len = dims
    n, d = z.shape
    tm = _row_tile(t_len, 272)
    row_spec = pl.BlockSpec((tm, d), lambda i: (i, 0))
    return pl.pallas_call(
        functools.partial(_norm_mod_kernel, tm=tm, t_len=t_len, c_len=c_len),
        out_shape=jax.ShapeDtypeStruct((n, d), BF16),
        grid=(n // tm,),
        in_specs=[row_spec, pl.BlockSpec((1, d), lambda i: (0, 0))] + _mod_specs(tm, t_len, n_batch, d, k_scale, k_shift),
        out_specs=row_spec,
        compiler_params=_cparams("parallel"), name="norm_mod",
    )(z, g.reshape(1, d), mod, mod, mod, mod)


def _norm_router(z, g, mod, k_shift, k_scale, dims, rw, rb):
    n_batch, t_len, c_len = dims
    n, d = z.shape
    n_exp = rw.shape[1]
    tm = max(t for t in (LANES, 2 * LANES) if t_len % t == 0)
    row_spec = pl.BlockSpec((tm, d), lambda i: (i, 0))
    lane_spec = pl.BlockSpec((tm, LANES), lambda i: (i, 0))
    return pl.pallas_call(
        functools.partial(_norm_router_kernel, tm=tm, t_len=t_len, c_len=c_len),
        out_shape=(jax.ShapeDtypeStruct((n, d // 2), jnp.uint32), jax.ShapeDtypeStruct((n, LANES), jnp.int32),
                   jax.ShapeDtypeStruct((n, LANES), jnp.int32), jax.ShapeDtypeStruct((n, LANES), F32),
                   jax.ShapeDtypeStruct((1, n_exp), F32)),
        grid=(n // tm,),
        in_specs=[row_spec, pl.BlockSpec((1, d), lambda i: (0, 0))] + _mod_specs(tm, t_len, n_batch, d, k_scale, k_shift)
        + [pl.BlockSpec((d, n_exp), lambda i: (0, 0)), pl.BlockSpec((1, n_exp), lambda i: (0, 0))],
        out_specs=(pl.BlockSpec((tm, d // 2), lambda i: (i, 0)), lane_spec, lane_spec, lane_spec,
                   pl.BlockSpec((1, n_exp), lambda i: (0, 0))),
        scratch_shapes=[pltpu.VMEM((1, n_exp), F32)],
        compiler_params=_cparams("arbitrary"), name="norm_router",
    )(z, g.reshape(1, d), mod, mod, mod, mod, rw, rb.reshape(1, n_exp))


def _final_norm_kernel(z_ref, g_ref, o_ref):
    o_ref[...] = _normed(z_ref, g_ref)


def _final_norm(z, g, dims):
    n_batch, t_len, c_len = dims
    n, d = z.shape
    l_len = t_len - c_len
    tm = math.gcd(c_len, l_len)
    per_b = l_len // tm
    off = c_len // tm
    return pl.pallas_call(
        _final_norm_kernel,
        out_shape=jax.ShapeDtypeStruct((n_batch * l_len, d), F32),
        grid=(n_batch, per_b),
        in_specs=[pl.BlockSpec((tm, d), lambda b, s: (b * (t_len // tm) + off + s, 0)),
                  pl.BlockSpec((1, d), lambda b, s: (0, 0))],
        out_specs=pl.BlockSpec((tm, d), lambda b, s: (b * per_b + s, 0)),
        compiler_params=_cparams("parallel", "parallel"), name="final_norm",
    )(z, g.reshape(1, d)).reshape(n_batch, l_len, d)


def _mm_kernel(x_ref, w_ref, o_ref):
    o_ref[...] = jnp.dot(x_ref[...], w_ref[...], preferred_element_type=F32).astype(o_ref.dtype)


def _matmul(x, w, out_dtype, tm_target=1088, tn_target=512):
    n, k = x.shape
    n_out = w.shape[1]
    tm = _row_tile(n, tm_target)
    tn = _col_tile(n_out, tn_target)
    return pl.pallas_call(
        _mm_kernel,
        out_shape=jax.ShapeDtypeStruct((n, n_out), out_dtype),
        grid=(n // tm, n_out // tn),
        in_specs=[pl.BlockSpec((tm, k), lambda i, j: (i, 0)),
                  pl.BlockSpec((k, tn), lambda i, j: (0, j))],
        out_specs=pl.BlockSpec((tm, tn), lambda i, j: (i, j)),
        compiler_params=_cparams("parallel", "arbitrary"), name="matmul",
    )(x, w)


def _mm_residual_kernel(x_ref, w_ref, z_ref, gl_ref, gc_ref, o_ref, *, tm, t_len, c_len):
    acc = jnp.dot(x_ref[...], w_ref[...], preferred_element_type=F32)
    gate = jnp.where(_is_ctx_rows(pl.program_id(0), tm, t_len, c_len), gc_ref[...], gl_ref[...])
    o_ref[...] = z_ref[...] + gate * acc


def _matmul_residual(x, w, z, mod, k_gate, dims, tm_target=1088, tn_target=512):
    n_batch, t_len, c_len = dims
    n, k = x.shape
    d = w.shape[1]
    tm = _row_tile(t_len, tm_target)
    tn = _col_tile(d, tn_target)
    return pl.pallas_call(
        functools.partial(_mm_residual_kernel, tm=tm, t_len=t_len, c_len=c_len),
        out_shape=jax.ShapeDtypeStruct((n, d), F32),
        grid=(n // tm, d // tn),
        in_specs=[pl.BlockSpec((tm, k), lambda i, j: (i, 0)),
                  pl.BlockSpec((k, tn), lambda i, j: (0, j)),
                  pl.BlockSpec((tm, tn), lambda i, j: (i, j)),
                  pl.BlockSpec((None, None, 1, tn), lambda i, j: ((i * tm) // t_len, k_gate, 0, j)),
                  pl.BlockSpec((None, None, 1, tn), lambda i, j: (n_batch, k_gate, 0, j))],
        out_specs=pl.BlockSpec((tm, tn), lambda i, j: (i, j)),
        input_output_aliases={2: 0},
        compiler_params=_cparams("parallel", "arbitrary"), name="matmul_residual",
    )(x, w, z, mod, mod)


def _glu_kernel(x_ref, w_ref, b_ref, y_ref, o_ref):
    acc = jnp.dot(x_ref[...], w_ref[...], preferred_element_type=F32) + b_ref[...]
    o_ref[...] = (y_ref[...].astype(F32) * jax.nn.sigmoid(acc)).astype(o_ref.dtype)


def _glu(y, w, b):
    n, k = y.shape
    tm = _row_tile(n, 1088)
    tn = _col_tile(k, 512)
    return pl.pallas_call(
        _glu_kernel,
        out_shape=jax.ShapeDtypeStruct((n, k), BF16),
        grid=(n // tm, k // tn),
        in_specs=[pl.BlockSpec((tm, k), lambda i, j: (i, 0)),
                  pl.BlockSpec((k, tn), lambda i, j: (0, j)),
                  pl.BlockSpec((1, tn), lambda i, j: (0, j)),
                  pl.BlockSpec((tm, tn), lambda i, j: (i, j))],
        out_specs=pl.BlockSpec((tm, tn), lambda i, j: (i, j)),
        compiler_params=_cparams("parallel", "arbitrary"), name="s5_glu",
    )(y, w, b.reshape(1, k), y)


def _merge_kernel(h_ref, ya_ref, yb_ref, yc_ref, wg_ref, bg_ref, wb_ref, o_ref):
    h = h_ref[...]
    acc = None
    for b, y_ref in enumerate((ya_ref, yb_ref, yc_ref)):
        gate = jax.nn.sigmoid(jnp.dot(h, wg_ref[b], preferred_element_type=F32) + bg_ref[b])
        term = gate * jnp.dot(y_ref[...], wb_ref[b], preferred_element_type=F32)
        acc = term if acc is None else acc + term
    o_ref[...] = acc.astype(o_ref.dtype)


def _merge(h, ya, yb, yc, w_gate, b_gate, w_branch, tm_target=544, tn_target=256):
    n, d = h.shape
    w = ya.shape[1]
    tm = _row_tile(n, tm_target)
    tn = _col_tile(d, tn_target)
    y_spec = pl.BlockSpec((tm, w), lambda i, j: (i, 0))
    return pl.pallas_call(
        _merge_kernel,
        out_shape=jax.ShapeDtypeStruct((n, d), BF16),
        grid=(n // tm, d // tn),
        in_specs=[pl.BlockSpec((tm, d), lambda i, j: (i, 0)), y_spec, y_spec, y_spec,
                  pl.BlockSpec((N_BRANCH, d, tn), lambda i, j: (0, 0, j)),
                  pl.BlockSpec((N_BRANCH, 1, tn), lambda i, j: (0, 0, j)),
                  pl.BlockSpec((N_BRANCH, w, tn), lambda i, j: (0, 0, j))],
        out_specs=pl.BlockSpec((tm, tn), lambda i, j: (i, j)),
        compiler_params=_cparams("parallel", "arbitrary"), name="merge",
    )(h, ya, yb, yc, w_gate, b_gate.reshape(N_BRANCH, 1, d), w_branch)


def _static_chunks(start, size, step):
    return [(s, min(step, start + size - s)) for s in range(start, start + size, step)]


def _scan_block_real(a, b, row, reverse):
    for s in (1, 2, 4):
        if reverse:
            keep = row < SUBLANES - s
            shift = SUBLANES - s
        else:
            keep = row >= s
            shift = s
        a_prev = jnp.where(keep, pltpu.roll(a, shift, 0), 1.0)
        b_prev = jnp.where(keep, pltpu.roll(b, shift, 0), 0.0)
        b = a * b_prev + b
        a = a * a_prev
    return a, b


def _lru_kernel(x_ref, g_ref, cw_ref, cb_ref, wa_ref, ba_ref, wx_ref, bx_ref, lam_ref, o_ref,
                xp_s, xc_s, a_s, b_s, y_s, *, c_len, l_len, rb):
    t_len = c_len + l_len
    width = x_ref.shape[1]
    cw = cw_ref[...]
    cb = cb_ref[...]
    zeros8 = jnp.zeros((SUBLANES, width), F32)
    for seg0, seg_n in ((0, c_len), (c_len, l_len)):
        xp_s[0:SUBLANES, :] = zeros8
        xp_s[SUBLANES + seg_n:2 * SUBLANES + seg_n, :] = zeros8
        for s0, sn in _static_chunks(0, seg_n, rb):
            xp_s[SUBLANES + s0:SUBLANES + s0 + sn, :] = x_ref[seg0 + s0:seg0 + s0 + sn, :]
        for s0, sn in _static_chunks(0, seg_n, rb):
            acc = cb + xp_s[SUBLANES - 1 + s0:SUBLANES - 1 + s0 + sn, :] * cw[0:1]
            for j in range(1, CONV_W):
                acc = acc + xp_s[SUBLANES - 1 + j + s0:SUBLANES - 1 + j + s0 + sn, :] * cw[j:j + 1]
            xc_s[seg0 + s0:seg0 + s0 + sn, :] = acc

    row = lax.broadcasted_iota(jnp.int32, (SUBLANES, width), 0)
    n_blk = t_len // SUBLANES
    c_blk = c_len // SUBLANES

    for d in range(2):
        wa = wa_ref[d]
        wx = wx_ref[d]
        ba = ba_ref[d:d + 1, :]
        bx = bx_ref[d:d + 1, :]
        sp = _softplus(-lam_ref[d:d + 1, :])

        def gates(r, carry, wa=wa, wx=wx, ba=ba, bx=bx, sp=sp):
            rows = pl.ds(pl.multiple_of(r * rb, rb), rb)
            xc = xc_s[rows, :]
            xb = xc.astype(BF16)
            rr = jax.nn.sigmoid(jnp.dot(xb, wa, preferred_element_type=F32) + ba)
            ii = jax.nn.sigmoid(jnp.dot(xb, wx, preferred_element_type=F32) + bx)
            log_a = (-LRU_C) * rr * sp
            th = jnp.tanh(log_a)
            a_s[rows, :] = jnp.exp(log_a)
            b_s[rows, :] = jnp.sqrt(-2.0 * th / (1.0 - th)) * (ii * xc)
            return carry
        lax.fori_loop(0, t_len // rb, gates, 0)

        reverse = d == 1

        def scan(i, h, reverse=reverse, first=(d == 0), lo=0, hi=n_blk):
            blk = (hi - 1 - i) if reverse else (lo + i)
            rows = pl.ds(pl.multiple_of(blk * SUBLANES, SUBLANES), SUBLANES)
            a_cum, b_cum = _scan_block_real(a_s[rows, :], b_s[rows, :], row, reverse)
            hs = a_cum * h + b_cum
            if first:
                y_s[rows, :] = hs
            else:
                y_s[rows, :] = y_s[rows, :] + hs
            return hs[0:1, :] if reverse else hs[SUBLANES - 1:SUBLANES, :]

        h0 = jnp.zeros((1, width), F32)
        if not reverse:
            lax.fori_loop(0, n_blk, scan, h0)
        else:
            h_ctx = lax.fori_loop(0, c_blk, functools.partial(scan, lo=0, hi=c_blk), h0)
            lax.fori_loop(0, n_blk - c_blk, functools.partial(scan, lo=c_blk, hi=n_blk), h_ctx)

    def finish(r, carry):
        rows = pl.ds(pl.multiple_of(r * rb, rb), rb)
        o_ref[rows, :] = (jax.nn.gelu(g_ref[rows, :]) * y_s[rows, :]).astype(o_ref.dtype)
        return carry
    lax.fori_loop(0, t_len // rb, finish, 0)


def _lru_branch(u, conv_w, conv_b, w_a, b_a, w_x, b_x, lam, dims):
    n_batch, t_len, c_len = dims
    l_len = t_len - c_len
    n = u.shape[0]
    width = conv_w.shape[1]
    bw = width // LRU_BLOCKS
    rb = math.gcd(math.gcd(c_len, l_len), 256)
    vec2 = pl.BlockSpec((2, bw), lambda b, k: (0, k))
    wspec = pl.BlockSpec((2, None, bw, bw), lambda b, k: (0, k, 0, 0))
    return pl.pallas_call(
        functools.partial(_lru_kernel, c_len=c_len, l_len=l_len, rb=rb),
        out_shape=jax.ShapeDtypeStruct((n, width), BF16),
        grid=(n_batch, LRU_BLOCKS),
        in_specs=[pl.BlockSpec((t_len, bw), lambda b, k: (b, k)),
                  pl.BlockSpec((t_len, bw), lambda b, k: (b, LRU_BLOCKS + k)),
                  pl.BlockSpec((CONV_W, bw), lambda b, k: (0, k)),
                  pl.BlockSpec((1, bw), lambda b, k: (0, k)),
                  wspec, vec2, wspec, vec2, vec2],
        out_specs=pl.BlockSpec((t_len, bw), lambda b, k: (b, k)),
        scratch_shapes=[pltpu.VMEM((l_len + 2 * SUBLANES, bw), F32),
                        pltpu.VMEM((t_len, bw), F32), pltpu.VMEM((t_len, bw), F32),
                        pltpu.VMEM((t_len, bw), F32), pltpu.VMEM((t_len, bw), F32)],
        compiler_params=_cparams("parallel", "parallel"), name="rglru",
    )(u, u, conv_w, conv_b.reshape(1, width), w_a.astype(BF16), b_a, w_x.astype(BF16), b_x, lam)


def _ret_kernel(q_ref, k_ref, v_ref, g_ref, cos_ref, sin_ref, dec_ref, o_ref,
                qs_s, ks_s, sb_s, dm_s, vec_s, *, c_len, l_len, rb):
    t_len = c_len + l_len
    hd = q_ref.shape[1]
    ch = RET_CHUNK
    k_scale = hd ** -0.5
    dec = dec_ref[...]
    lg_f = -_softplus(-dec[0])
    lg_b = -_softplus(-dec[1])
    ri = lax.broadcasted_iota(jnp.int32, (ch, ch), 0).astype(F32)
    ci = lax.broadcasted_iota(jnp.int32, (ch, ch), 1).astype(F32)
    diff = ri - ci
    dm_s[...] = (jnp.where(diff >= 0, jnp.exp(lg_f * jnp.maximum(diff, 0.0)), 0.0)
                 + jnp.where(diff <= 0, jnp.exp(lg_b * jnp.maximum(-diff, 0.0)), 0.0))
    pos = lax.broadcasted_iota(jnp.int32, (ch, hd), 0).astype(F32)
    vec_s[0] = jnp.exp(lg_f * (ch - 1.0 - pos))
    vec_s[1] = jnp.exp(lg_f * (pos + 1.0))
    vec_s[2] = jnp.exp(lg_b * pos)
    vec_s[3] = jnp.exp(lg_b * (ch - pos))
    cd_f = jnp.exp(lg_f * float(ch))
    cd_b = jnp.exp(lg_b * float(ch))

    for s0, sn in _static_chunks(0, c_len, rb):
        qs_s[s0:s0 + sn, :] = q_ref[s0:s0 + sn, :]
        ks_s[s0:s0 + sn, :] = k_ref[s0:s0 + sn, :] * k_scale
    lane = lax.broadcasted_iota(jnp.int32, (rb, hd), 1)
    low = (lane % (hd // 2)) < (hd // 4)

    def rope(r, carry):
        src = pl.ds(pl.multiple_of(c_len + r * rb, rb), rb)
        tab = pl.ds(pl.multiple_of(r * rb, rb), rb)
        cs = cos_ref[tab, :]
        sn = sin_ref[tab, :]
        for ref, dst, scale in ((q_ref, qs_s, 1.0), (k_ref, ks_s, k_scale)):
            x = ref[src, :]
            partner = jnp.where(low, pltpu.roll(x, hd - hd // 4, 1), pltpu.roll(x, hd // 4, 1))
            y = x * cs + partner * sn
            dst[src, :] = y * scale if scale != 1.0 else y
        return carry
    lax.fori_loop(0, l_len // rb, rope, 0)

    def chunk_rows(c):
        return pl.ds(pl.multiple_of(c * ch, ch), ch)

    def kv_state(rows, kdec):
        kd = (ks_s[rows, :] * kdec).T.astype(BF16)
        return jnp.dot(kd, v_ref[rows, :].astype(BF16), preferred_element_type=F32)

    n_ch = t_len // ch
    c_ch = c_len // ch

    def back(i, s, lo, hi):
        c = hi - 1 - i
        sb_s[c] = s
        return cd_b * s + kv_state(chunk_rows(c), vec_s[2])
    s0 = jnp.zeros((hd, hd), F32)
    s_ctx = lax.fori_loop(0, c_ch, functools.partial(back, lo=0, hi=c_ch), s0)
    lax.fori_loop(0, n_ch - c_ch, functools.partial(back, lo=c_ch, hi=n_ch), s_ctx)

    def fwd(c, s):
        rows = chunk_rows(c)
        q = qs_s[rows, :]
        qb = q.astype(BF16)
        kb = ks_s[rows, :].astype(BF16)
        vb = v_ref[rows, :].astype(BF16)
        scores = lax.dot_general(qb, kb, (((1,), (1,)), ((), ())), preferred_element_type=F32) * dm_s[...]
        o = jnp.dot(scores.astype(BF16), vb, preferred_element_type=F32)
        o = o + jnp.dot((q * vec_s[1]).astype(BF16), s.astype(BF16), preferred_element_type=F32)
        o = o + jnp.dot((q * vec_s[3]).astype(BF16), sb_s[c].astype(BF16), preferred_element_type=F32)
        o = o * lax.rsqrt(jnp.mean(o * o, axis=-1, keepdims=True) + NORM_EPS)
        g = g_ref[rows, :]
        o_ref[rows, :] = (o * (g * jax.nn.sigmoid(g))).astype(o_ref.dtype)
        return cd_f * s + kv_state(rows, vec_s[0])
    lax.fori_loop(0, n_ch, fwd, s0)


def _rope_tables(l_len, hd):
    t = jnp.arange(l_len, dtype=jnp.int32)
    row = (t // GRID_W).astype(F32)
    col = (t % GRID_W).astype(F32)
    quarter = hd // 4
    freqs = ROPE_BASE ** (-jnp.arange(quarter, dtype=F32) / quarter)
    ang_r = row[:, None] * freqs
    ang_c = col[:, None] * freqs
    cos = jnp.concatenate([jnp.cos(ang_r)] * 2 + [jnp.cos(ang_c)] * 2, axis=-1)
    sin = jnp.concatenate([-jnp.sin(ang_r), jnp.sin(ang_r), -jnp.sin(ang_c), jnp.sin(ang_c)], axis=-1)
    return cos, sin


def _ret_branch(u, cos, sin, decay, dims):
    n_batch, t_len, c_len = dims
    l_len = t_len - c_len
    n = u.shape[0]
    hd = RET_HEAD_DIM
    n_heads = decay.shape[1]
    rb = math.gcd(math.gcd(c_len, l_len), 256)

    def col(first):
        return pl.BlockSpec((t_len, hd), lambda b, h: (b, first + h))
    tab = pl.BlockSpec((l_len, hd), lambda b, h: (0, 0))
    return pl.pallas_call(
        functools.partial(_ret_kernel, c_len=c_len, l_len=l_len, rb=rb),
        out_shape=jax.ShapeDtypeStruct((n, n_heads * hd), BF16),
        grid=(n_batch, n_heads),
        in_specs=[col(2 * n_heads), col(3 * n_heads), col(4 * n_heads), col(5 * n_heads), tab, tab,
                  pl.BlockSpec((2, None, 1, 1), lambda b, h: (0, h, 0, 0))],
        out_specs=pl.BlockSpec((t_len, hd), lambda b, h: (b, h)),
        scratch_shapes=[pltpu.VMEM((t_len, hd), F32), pltpu.VMEM((t_len, hd), F32),
                        pltpu.VMEM((t_len // RET_CHUNK, hd, hd), F32),
                        pltpu.VMEM((RET_CHUNK, RET_CHUNK), F32),
                        pltpu.VMEM((4, RET_CHUNK, hd), F32)],
        compiler_params=_cparams("parallel", "parallel"), name="retention",
    )(u, u, u, u, cos, sin, decay.reshape(2, n_heads, 1, 1))


def _s5_disc_kernel(are_ref, aim_ref, ldt_ref, bre_ref, bim_ref, pre_ref, pim_ref, bbre_ref, bbim_ref):
    lam_re = jnp.minimum(are_ref[...], -1e-4)
    lam_im = aim_ref[...]
    dt = jnp.exp(ldt_ref[...])
    z_re = lam_re * dt
    z_im = lam_im * dt
    mag = jnp.exp(z_re)
    ab_re = mag * jnp.cos(z_im)
    ab_im = mag * jnp.sin(z_im)
    den = lam_re * lam_re + lam_im * lam_im
    n_re = ab_re - 1.0
    co_re = (n_re * lam_re + ab_im * lam_im) / den
    co_im = (ab_im * lam_re - n_re * lam_im) / den
    for i in range(bre_ref.shape[0]):
        b_re = bre_ref[i]
        b_im = bim_ref[i]
        bbre_ref[i] = co_re * b_re - co_im * b_im
        bbim_ref[i] = co_re * b_im + co_im * b_re
    p_re, p_im = ab_re, ab_im
    pre_ref[0] = p_re
    pim_ref[0] = p_im
    for j in range(1, SUBLANES):
        p_re, p_im = p_re * ab_re - p_im * ab_im, p_re * ab_im + p_im * ab_re
        pre_ref[j] = p_re
        pim_ref[j] = p_im


def _s5_discretise(a_re, a_im, log_dt, b_re, b_im):
    two, g, p = a_re.shape
    i = b_re.shape[-1]
    rows = two * g
    full2 = pl.BlockSpec((rows, p), lambda: (0, 0))
    full3 = pl.BlockSpec((i, rows, p), lambda: (0, 0, 0))
    pw = pl.BlockSpec((SUBLANES, rows, p), lambda: (0, 0, 0))

    def input_major(m):
        return jnp.transpose(m, (3, 0, 1, 2)).reshape(i, rows, p)
    pw_re, pw_im, bb_re, bb_im = pl.pallas_call(
        _s5_disc_kernel,
        out_shape=(jax.ShapeDtypeStruct((SUBLANES, rows, p), F32), jax.ShapeDtypeStruct((SUBLANES, rows, p), F32),
                   jax.ShapeDtypeStruct((i, rows, p), F32), jax.ShapeDtypeStruct((i, rows, p), F32)),
        in_specs=[full2, full2, pl.BlockSpec((rows, 1), lambda: (0, 0)), full3, full3],
        out_specs=(pw, pw, full3, full3),
        name="s5_discretise",
    )(a_re.reshape(rows, p), a_im.reshape(rows, p), log_dt.reshape(rows, 1), input_major(b_re), input_major(b_im))
    return pw_re, pw_im, jnp.swapaxes(bb_re, 0, 1), jnp.swapaxes(bb_im, 0, 1)


def _s5_kernel(u_ref, bb_ref, cb_ref, pw_ref, dsk_ref, o_ref, hr_s, hi_s, y_s, lv_s, *, c_len, l_len, rb):
    t_len = c_len + l_len
    ns = hr_s.shape[1]
    row = lax.broadcasted_iota(jnp.int32, (SUBLANES, ns), 0)
    n_blk = t_len // SUBLANES
    c_blk = c_len // SUBLANES
    dsk = dsk_ref[...]

    def skip(r, carry):
        rows = pl.ds(pl.multiple_of(r * rb, rb), rb)
        y_s[rows, :] = dsk * u_ref[rows, :]
        return carry
    lax.fori_loop(0, t_len // rb, skip, 0)

    for d in range(2):
        reverse = d == 1
        bb = bb_ref[d]

        def drive(r, carry, bb=bb):
            rows = pl.ds(pl.multiple_of(r * rb, rb), rb)
            bu = jnp.dot(u_ref[rows, :].astype(BF16), bb, preferred_element_type=F32)
            hr_s[rows, :] = bu[:, :ns]
            hi_s[rows, :] = bu[:, ns:]
            return carry
        lax.fori_loop(0, t_len // rb, drive, 0)

        for lvl, s in enumerate((1, 2, 4)):
            keep = (row < SUBLANES - s) if reverse else (row >= s)
            for part in range(2):
                lv_s[lvl, part] = jnp.where(keep, pw_ref[d, part, s - 1:s, :], 0.0)
        for part in range(2):
            if reverse:
                for j in range(SUBLANES):
                    lv_s[3, part, j:j + 1, :] = pw_ref[d, part, SUBLANES - 1 - j:SUBLANES - j, :]
            else:
                lv_s[3, part] = pw_ref[d, part]

        def scan(i, carry, reverse=reverse, lo=0, hi=n_blk):
            cr, ci = carry
            blk = (hi - 1 - i) if reverse else (lo + i)
            rows = pl.ds(pl.multiple_of(blk * SUBLANES, SUBLANES), SUBLANES)
            hr = hr_s[rows, :]
            hi_ = hi_s[rows, :]
            for lvl, s in enumerate((1, 2, 4)):
                shift = SUBLANES - s if reverse else s
                ar = lv_s[lvl, 0]
                ai = lv_s[lvl, 1]
                sr = pltpu.roll(hr, shift, 0)
                si = pltpu.roll(hi_, shift, 0)
                hr, hi_ = hr + (ar * sr - ai * si), hi_ + (ar * si + ai * sr)
            pr = lv_s[3, 0]
            pi = lv_s[3, 1]
            hr, hi_ = hr + (pr * cr - pi * ci), hi_ + (pr * ci + pi * cr)
            hr_s[rows, :] = hr
            hi_s[rows, :] = hi_
            if reverse:
                return hr[0:1, :], hi_[0:1, :]
            return hr[SUBLANES - 1:SUBLANES, :], hi_[SUBLANES - 1:SUBLANES, :]

        zero = (jnp.zeros((1, ns), F32), jnp.zeros((1, ns), F32))
        if not reverse:
            lax.fori_loop(0, n_blk, scan, zero)
        else:
            mid = lax.fori_loop(0, c_blk, functools.partial(scan, lo=0, hi=c_blk), zero)
            lax.fori_loop(0, n_blk - c_blk, functools.partial(scan, lo=c_blk, hi=n_blk), mid)

        cb = cb_ref[d]

        def readout(r, carry, cb=cb):
            rows = pl.ds(pl.multiple_of(r * rb, rb), rb)
            y = jnp.dot(hr_s[rows, :].astype(BF16), cb[:ns, :], preferred_element_type=F32)
            y = y + jnp.dot(hi_s[rows, :].astype(BF16), cb[ns:, :], preferred_element_type=F32)
            y_s[rows, :] = y_s[rows, :] + y
            return carry
        lax.fori_loop(0, t_len // rb, readout, 0)

    def finish(r, carry):
        rows = pl.ds(pl.multiple_of(r * rb, rb), rb)
        o_ref[rows, :] = jax.nn.gelu(y_s[rows, :]).astype(o_ref.dtype)
        return carry
    lax.fori_loop(0, t_len // rb, finish, 0)


def _s5_branch(u, a_re, a_im, log_dt, b_re, b_im, c_re, c_im, d_skip, dims):
    n_batch, t_len, c_len = dims
    l_len = t_len - c_len
    n = u.shape[0]
    _, g, p = a_re.shape
    i = b_re.shape[-1]
    width = g * i
    gpb = LANES // i
    nb = g // gpb
    ns = gpb * p
    pw_re, pw_im, bb_re, bb_im = _s5_discretise(a_re, a_im, log_dt, b_re, b_im)
    eye = jnp.eye(gpb, dtype=F32)

    def blockdiag_in(m):
        m = m.reshape(2, nb, gpb, i, p)
        return jnp.einsum('dbgip,gh->dbgihp', m, eye).reshape(2, nb, gpb * i, ns)
    bb = jnp.concatenate([blockdiag_in(bb_re), blockdiag_in(bb_im)], axis=-1).astype(BF16)

    def blockdiag_out(m):
        m = m.reshape(2, nb, gpb, i, p)
        return jnp.einsum('dbgip,gh->dbgphi', m, eye).reshape(2, nb, ns, gpb * i)
    cb = jnp.concatenate([blockdiag_out(c_re), -blockdiag_out(c_im)], axis=-2).astype(BF16)
    pw = jnp.stack([pw_re.reshape(SUBLANES, 2, g * p), pw_im.reshape(SUBLANES, 2, g * p)], axis=0)
    pw = jnp.transpose(pw, (2, 0, 1, 3))
    rb = math.gcd(math.gcd(c_len, l_len), 256)
    cols = (N_IN_SPLITS - 1) * width // LANES
    return pl.pallas_call(
        functools.partial(_s5_kernel, c_len=c_len, l_len=l_len, rb=rb),
        out_shape=jax.ShapeDtypeStruct((n, width), BF16),
        grid=(n_batch, nb),
        in_specs=[pl.BlockSpec((t_len, LANES), lambda b, k: (b, cols + k)),
                  pl.BlockSpec((2, None, LANES, 2 * ns), lambda b, k: (0, k, 0, 0)),
                  pl.BlockSpec((2, None, 2 * ns, LANES), lambda b, k: (0, k, 0, 0)),
                  pl.BlockSpec((2, 2, SUBLANES, ns), lambda b, k: (0, 0, 0, k)),
                  pl.BlockSpec((1, LANES), lambda b, k: (0, k))],
        out_specs=pl.BlockSpec((t_len, LANES), lambda b, k: (b, k)),
        scratch_shapes=[pltpu.VMEM((t_len, ns), F32), pltpu.VMEM((t_len, ns), F32),
                        pltpu.VMEM((t_len, LANES), F32),
                        pltpu.VMEM((4, 2, SUBLANES, ns), F32)],
        compiler_params=_cparams("parallel", "parallel"), name="s5",
    )(u, bb, cb, pw, d_skip.reshape(1, width))


EXPERT_TILE = 256
GATHER_UNROLL = 8


def _expert_kernel(te_ref, nu_ref, tok_ref, hp_hbm, w1_ref, b1_ref, w2_ref, b2_ref, y_ref, xbuf, xb, sem,
                   *, tme, ff, dc):
    j = pl.program_id(0)
    n_used = nu_ref[0]
    slot = j % 2
    half = xbuf.shape[2]

    def row_copy(tile, r, s):
        tok = tok_ref[tile * tme + r]
        return pltpu.make_async_copy(hp_hbm.at[pl.ds(tok, 1), :], xbuf.at[s, pl.ds(r, 1), :], sem.at[s])

    def start_tile(tile, s):
        def body(r, c):
            row_copy(tile, r, s).start()
            return c
        lax.fori_loop(0, tme, body, 0, unroll=GATHER_UNROLL)

    @pl.when(j == 0)
    def _():
        start_tile(j, slot)

    @pl.when(j + 1 < n_used)
    def _():
        start_tile(j + 1, 1 - slot)

    @pl.when(j < n_used)
    def _():
        def body(r, c):
            row_copy(j, r, slot).wait()
            return c
        lax.fori_loop(0, tme, body, 0, unroll=GATHER_UNROLL)
        rc = 2 * SUBLANES

        def unpack(q, c):
            rows = pl.ds(pl.multiple_of(q * rc, rc), rc)
            for c0 in range(0, half, COMBINE_COLS):
                lo, hi = _unpack_bf16_pairs(xbuf[slot, rows, c0:c0 + COMBINE_COLS])
                xb[rows, c0:c0 + COMBINE_COLS] = lo.astype(BF16)
                xb[rows, half + c0:half + c0 + COMBINE_COLS] = hi.astype(BF16)
            return c
        lax.fori_loop(0, tme // rc, unpack, 0)
        hu = jnp.dot(xb[...], w1_ref[...], preferred_element_type=F32) + b1_ref[...]
        gate = jnp.minimum(hu[:, :ff], SWIGLU_LIMIT)
        up = jnp.clip(hu[:, ff:], -SWIGLU_LIMIT, SWIGLU_LIMIT)
        act = (gate * jax.nn.sigmoid(SWIGLU_ALPHA * gate) * (up + 1.0)).astype(BF16)
        for c0 in range(0, half, dc):
            lo = jnp.dot(act, w2_ref[:, c0:c0 + dc], preferred_element_type=F32) + b2_ref[:, c0:c0 + dc]
            hi = (jnp.dot(act, w2_ref[:, half + c0:half + c0 + dc], preferred_element_type=F32)
                  + b2_ref[:, half + c0:half + c0 + dc])
            y_ref[:, c0:c0 + dc] = _pack_halves(lo, hi)

    @pl.when(j >= n_used)
    def _():
        y_ref[...] = jnp.zeros(y_ref.shape, y_ref.dtype)


def _experts(hp, tile_expert, n_used, tok_of_slot, w1, b1, w2, b2):
    n_exp, d, ff2 = w1.shape
    half = d // 2
    n_tiles = tile_expert.shape[0]
    tme = EXPERT_TILE
    grid_spec = pltpu.PrefetchScalarGridSpec(
        num_scalar_prefetch=3,
        grid=(n_tiles,),
        in_specs=[pl.BlockSpec(memory_space=pl.ANY),
                  pl.BlockSpec((None, d, ff2), lambda j, te, nu, tok: (te[j], 0, 0)),
                  pl.BlockSpec((None, 1, ff2), lambda j, te, nu, tok: (te[j], 0, 0)),
                  pl.BlockSpec((None, ff2 // 2, d), lambda j, te, nu, tok: (te[j], 0, 0)),
                  pl.BlockSpec((None, 1, d), lambda j, te, nu, tok: (te[j], 0, 0))],
        out_specs=pl.BlockSpec((tme, half), lambda j, te, nu, tok: (j, 0)),
        scratch_shapes=[pltpu.VMEM((2, tme, half), jnp.uint32), pltpu.VMEM((tme, d), BF16),
                        pltpu.SemaphoreType.DMA((2,))],
    )
    return pl.pallas_call(
        functools.partial(_expert_kernel, tme=tme, ff=ff2 // 2, dc=_col_tile(half, 512)),
        out_shape=jax.ShapeDtypeStruct((n_tiles * tme, half), jnp.uint32),
        grid_spec=grid_spec,
        compiler_params=_cparams("arbitrary"), name="experts",
    )(tile_expert, n_used, tok_of_slot, hp, w1, b1.reshape(n_exp, 1, ff2), w2, b2.reshape(n_exp, 1, d))


COMBINE_ROWS = SUBLANES
COMBINE_COLS = 4 * LANES


def _combine_kernel(pos_ref, y_hbm, prob_ref, z_ref, gl_ref, gc_ref, o_ref, ybuf, sem, *, tm, t_len, c_len):
    i = pl.program_id(0)
    n_steps = pl.num_programs(0)
    slot = i % 2
    half = ybuf.shape[3]

    def row_copy(tile, r, k, s):
        row = pos_ref[(tile * tm + r) * TOP_K + k]
        return pltpu.make_async_copy(y_hbm.at[pl.ds(row, 1), :], ybuf.at[s, k, pl.ds(r, 1), :], sem.at[s])

    def start_tile(tile, s):
        def body(r, c):
            for k in range(TOP_K):
                row_copy(tile, r, k, s).start()
            return c
        lax.fori_loop(0, tm, body, 0, unroll=GATHER_UNROLL // TOP_K)

    @pl.when(i == 0)
    def _():
        start_tile(i, slot)

    @pl.when(i + 1 < n_steps)
    def _():
        start_tile(i + 1, 1 - slot)

    def wait_body(r, c):
        for k in range(TOP_K):
            row_copy(i, r, k, slot).wait()
        return c
    lax.fori_loop(0, tm, wait_body, 0, unroll=GATHER_UNROLL // TOP_K)

    rc = COMBINE_ROWS

    cw = COMBINE_COLS

    def reduce_rows(q, c):
        rows = pl.ds(pl.multiple_of(q * rc, rc), rc)
        prob = prob_ref[rows, :]
        row = (i * tm) % t_len + q * rc + lax.broadcasted_iota(jnp.int32, (rc, 1), 0)
        is_ctx = row < c_len
        for c0 in range(0, half, cw):
            acc_lo = acc_hi = None
            for k in range(TOP_K):
                lo, hi = _unpack_bf16_pairs(ybuf[slot, k, rows, c0:c0 + cw])
                p = prob[:, k:k + 1]
                acc_lo = p * lo if acc_lo is None else acc_lo + p * lo
                acc_hi = p * hi if acc_hi is None else acc_hi + p * hi
            for acc, off in ((acc_lo, c0), (acc_hi, half + c0)):
                gate = jnp.where(is_ctx, gc_ref[:, off:off + cw], gl_ref[:, off:off + cw])
                o_ref[rows, off:off + cw] = z_ref[rows, off:off + cw] + gate * acc
        return c
    lax.fori_loop(0, tm // rc, reduce_rows, 0)


def _combine(y_sorted, pos, prob, z, mod, k_gate, dims):
    n_batch, t_len, c_len = dims
    n, d = z.shape
    half = d // 2
    tm = max(t for t in (LANES, 2 * LANES) if t_len % t == 0)
    grid_spec = pltpu.PrefetchScalarGridSpec(
        num_scalar_prefetch=1,
        grid=(n // tm,),
        in_specs=[pl.BlockSpec(memory_space=pl.ANY),
                  pl.BlockSpec((tm, LANES), lambda i, pos: (i, 0)),
                  pl.BlockSpec((tm, d), lambda i, pos: (i, 0)),
                  pl.BlockSpec((None, None, 1, d), lambda i, pos: ((i * tm) // t_len, k_gate, 0, 0)),
                  pl.BlockSpec((None, None, 1, d), lambda i, pos: (n_batch, k_gate, 0, 0))],
        out_specs=pl.BlockSpec((tm, d), lambda i, pos: (i, 0)),
        scratch_shapes=[pltpu.VMEM((2, TOP_K, tm, half), jnp.uint32), pltpu.SemaphoreType.DMA((2,))],
    )
    return pl.pallas_call(
        functools.partial(_combine_kernel, tm=tm, t_len=t_len, c_len=c_len),
        out_shape=jax.ShapeDtypeStruct((n, d), F32),
        grid_spec=grid_spec,
        input_output_aliases={3: 0},
        compiler_params=_cparams("arbitrary"), name="moe_combine",
    )(pos, y_sorted, prob, z, mod, mod)


def _route(idx, rank, counts, n_exp):
    n = idx.shape[0]
    tme = EXPERT_TILE
    n_tiles = (n * TOP_K + n_exp * (tme - 1)) // tme + 1
    counts = counts.reshape(n_exp).astype(jnp.int32)
    padded = ((counts + tme - 1) // tme) * tme
    ends = jnp.cumsum(padded)
    base = ends - padded
    idx = idx[:, :TOP_K]
    pos = jnp.take(base, idx) + rank[:, :TOP_K]
    tile_expert = jnp.searchsorted(ends, jnp.arange(n_tiles, dtype=jnp.int32) * tme, side='right')
    tile_expert = jnp.minimum(tile_expert, n_exp - 1).astype(jnp.int32)
    n_used = (ends[-1:] // tme).astype(jnp.int32)
    token = jnp.broadcast_to(jnp.arange(n, dtype=jnp.int32)[:, None], (n, TOP_K))
    tok_of_slot = jnp.zeros((n_tiles * tme,), jnp.int32).at[pos.reshape(-1)].set(
        token.reshape(-1), unique_indices=True, indices_are_sorted=False)
    return pos.reshape(-1).astype(jnp.int32), tile_expert, n_used, tok_of_slot


def _moe_sublayer(z, g, mod, dims, rw, rb, w1, b1, w2, b2):
    hp, idx, rank, prob, counts = _norm_router(z, g, mod, 3, 4, dims, rw, rb)
    pos, tile_expert, n_used, tok_of_slot = _route(idx, rank, counts, rw.shape[1])
    y_sorted = _experts(hp, tile_expert, n_used, tok_of_slot, w1, b1, w2, b2)
    return _combine(y_sorted, pos, prob, z, mod, 5, dims)


def _layer(z, p, cos, sin, dims):
    d = z.shape[1]
    mod = p['mod'].reshape(p['mod'].shape[0], N_MOD, 1, d)
    h = _norm_mod(z, p['norm_mix_g'], mod, 0, 1, dims)
    u = _matmul(h, p['w_in'].astype(BF16), F32)
    ya = _lru_branch(u, p['conv_w'], p['conv_b'], p['lru_w_a'], p['lru_b_a'], p['lru_w_x'], p['lru_b_x'],
                     p['lru_lam'], dims)
    yb = _ret_branch(u, cos, sin, p['ret_decay'], dims)
    yc = _s5_branch(u, p['s5_a_re'], p['s5_a_im'], p['s5_log_dt'], p['s5_b_re'], p['s5_b_im'],
                    p['s5_c_re'], p['s5_c_im'], p['s5_d'], dims)
    yc = _glu(yc, p['s5_w_glu'].astype(BF16), p['s5_b_glu'])
    m = _merge(h, ya, yb, yc, p['w_gate'].astype(BF16), p['b_gate'], p['w_branch'].astype(BF16))
    z = _matmul_residual(m, p['w_out'].astype(BF16), z, mod, 2, dims)
    return _moe_sublayer(z, p['norm_ffn_g'], mod, dims, p['router_w'], p['router_b'],
                         p['moe_w1'].astype(BF16), p['moe_b1'], p['moe_w2'].astype(BF16), p['moe_b2'])


def kernel(x, c, ctx, c_ctx, mod_w_a, mod_w_b, mod_b, norm_mix_g, norm_ffn_g, w_in, conv_w, conv_b, lru_w_a, lru_b_a, lru_w_x, lru_b_x, lru_lam, ret_decay, s5_a_re, s5_a_im, s5_log_dt, s5_b_re, s5_b_im, s5_c_re, s5_c_im, s5_d, s5_w_glu, s5_b_glu, w_branch, w_gate, b_gate, w_out, router_w, router_b, moe_w1, moe_b1, moe_w2, moe_b2, final_norm_g):
    n_batch, l_len, d = x.shape
    c_len = ctx.shape[1]
    t_len = c_len + l_len
    dims = (n_batch, t_len, c_len)
    z = jnp.concatenate([ctx, x], axis=1).reshape(n_batch * t_len, d)
    pad = (-(n_batch + 1)) % SUBLANES
    cc = jnp.concatenate([c, c_ctx[None, :], jnp.zeros((pad, d), F32)], axis=0)
    mod = _modulation(cc, mod_w_a, mod_w_b, mod_b)
    cos, sin = _rope_tables(l_len, RET_HEAD_DIM)
    params = dict(mod=mod, norm_mix_g=norm_mix_g, norm_ffn_g=norm_ffn_g, w_in=w_in, conv_w=conv_w, conv_b=conv_b,
                  lru_w_a=lru_w_a, lru_b_a=lru_b_a, lru_w_x=lru_w_x, lru_b_x=lru_b_x, lru_lam=lru_lam,
                  ret_decay=ret_decay, s5_a_re=s5_a_re, s5_a_im=s5_a_im, s5_log_dt=s5_log_dt, s5_b_re=s5_b_re,
                  s5_b_im=s5_b_im, s5_c_re=s5_c_re, s5_c_im=s5_c_im, s5_d=s5_d, s5_w_glu=s5_w_glu,
                  s5_b_glu=s5_b_glu, w_branch=w_branch, w_gate=w_gate, b_gate=b_gate, w_out=w_out,
                  router_w=router_w, router_b=router_b, moe_w1=moe_w1, moe_b1=moe_b1, moe_w2=moe_w2,
                  moe_b2=moe_b2)

    def body(zc, p):
        return _layer(zc, p, cos, sin, dims), None
    z, _ = lax.scan(body, z, params)
    return _final_norm(z, final_norm_g, dims)
```

```python
import functools
import math

import jax
import jax.numpy as jnp
from jax import lax
from jax.experimental import pallas as pl
from jax.experimental.pallas import tpu as pltpu

F32 = jnp.float32
BF16 = jnp.bfloat16
HIGHEST = lax.Precision.HIGHEST

V7X_VMEM_BYTES = 64 * 1024 * 1024
VMEM_LIMIT = V7X_VMEM_BYTES - 8 * 1024 * 1024
SUBLANES = 8
LANES = 128

N_IN_SPLITS = 7
N_BRANCH = 3
N_MOD = 6
LRU_BLOCKS = 8
LRU_C = 8.0
CONV_W = 4
RET_HEAD_DIM = 128
RET_CHUNK = 128
ROPE_BASE = 10000.0
GRID_W = 64
S5_IN = 16
S5_STATE = 64
TOP_K = 4
SWIGLU_LIMIT = 7.0
SWIGLU_ALPHA = 1.702
NORM_EPS = 1e-6


def _cparams(*sem):
    return pltpu.CompilerParams(dimension_semantics=sem, vmem_limit_bytes=VMEM_LIMIT)


def _row_tile(t, target):
    best = None
    for d in range(16, min(t, target) + 1, 16):
        if t % d == 0:
            best = d
    assert best is not None, (t, target)
    return best


def _col_tile(n, target):
    best = None
    for d in range(LANES, min(n, target) + 1, LANES):
        if n % d == 0:
            best = d
    assert best is not None, (n, target)
    return best


def _softplus(x):
    return jnp.maximum(x, 0.0) + jnp.log1p(jnp.exp(-jnp.abs(x)))


def _is_ctx_rows(i, tm, t_len, c_len):
    row = (i * tm) % t_len + lax.broadcasted_iota(jnp.int32, (tm, 1), 0)
    return row < c_len


def _mod_kernel(cc_ref, wa_ref, wb_ref, b_ref, o_ref):
    cc = cc_ref[...]
    s = cc * jax.nn.sigmoid(cc)
    t = jnp.dot(s, wa_ref[...], precision=HIGHEST, preferred_element_type=F32)
    o_ref[...] = jnp.dot(t, wb_ref[...], precision=HIGHEST, preferred_element_type=F32) + b_ref[...]


def _modulation(cc, mod_w_a, mod_w_b, mod_b):
    depth, d, r = mod_w_a.shape
    rows = cc.shape[0]
    return pl.pallas_call(
        _mod_kernel,
        out_shape=jax.ShapeDtypeStruct((depth, rows, N_MOD * d), F32),
        grid=(depth, N_MOD),
        in_specs=[
            pl.BlockSpec((rows, d), lambda l, j: (0, 0)),
            pl.BlockSpec((None, d, r), lambda l, j: (l, 0, 0)),
            pl.BlockSpec((None, r, d), lambda l, j: (l, 0, j)),
            pl.BlockSpec((None, 1, d), lambda l, j: (l, 0, j)),
        ],
        out_specs=pl.BlockSpec((None, rows, d), lambda l, j: (l, 0, j)),
        compiler_params=_cparams("arbitrary", "arbitrary"),
        name="modulation",
    )(cc, mod_w_a, mod_w_b, mod_b.reshape(depth, 1, N_MOD * d))


def _normed(z_ref, g_ref):
    x = z_ref[...]
    return x * lax.rsqrt(jnp.mean(x * x, axis=-1, keepdims=True) + NORM_EPS) * g_ref[...]


def _norm_mod_kernel(z_ref, g_ref, scl_ref, shl_ref, scc_ref, shc_ref, o_ref, *, tm, t_len, c_len):
    y = _normed(z_ref, g_ref)
    is_ctx = _is_ctx_rows(pl.program_id(0), tm, t_len, c_len)
    scale = jnp.where(is_ctx, scc_ref[...], scl_ref[...])
    shift = jnp.where(is_ctx, shc_ref[...], shl_ref[...])
    o_ref[...] = (y * (1.0 + scale) + shift).astype(o_ref.dtype)


HIGH_HALF = 0xFFFF0000


def _pack_halves(lo, hi):
    lo_bits = pltpu.bitcast(lo.astype(BF16).astype(F32), jnp.uint32)
    hi_bits = pltpu.bitcast(hi.astype(BF16).astype(F32), jnp.uint32)
    return (lo_bits >> jnp.uint32(16)) | (hi_bits & jnp.uint32(HIGH_HALF))


def _pack_bf16_pairs(x):
    half = x.shape[1] // 2
    return _pack_halves(x[:, :half], x[:, half:])


def _unpack_bf16_pairs(w):
    return pltpu.bitcast(w << jnp.uint32(16), F32), pltpu.bitcast(w & jnp.uint32(HIGH_HALF), F32)


def _norm_router_kernel(z_ref, g_ref, scl_ref, shl_ref, scc_ref, shc_ref, rw_ref, rb_ref,
                        hp_ref, idx_ref, rank_ref, prob_ref, cnt_ref, cnt_s, *, tm, t_len, c_len):
    i = pl.program_id(0)

    @pl.when(i == 0)
    def _():
        cnt_s[...] = jnp.zeros(cnt_s.shape, F32)
    y = _normed(z_ref, g_ref)
    is_ctx = _is_ctx_rows(i, tm, t_len, c_len)
    scale = jnp.where(is_ctx, scc_ref[...], scl_ref[...])
    shift = jnp.where(is_ctx, shc_ref[...], shl_ref[...])
    h = y * (1.0 + scale) + shift
    hp_ref[...] = _pack_bf16_pairs(h)
    logits = jnp.dot(h, rw_ref[...], precision=HIGHEST, preferred_element_type=F32) + rb_ref[...]
    n_exp = logits.shape[1]
    lane = lax.broadcasted_iota(jnp.int32, logits.shape, 1).astype(F32)
    work = logits
    picked = jnp.zeros(logits.shape, F32)
    firsts, vals = [], []
    for k in range(TOP_K):
        m = jnp.max(work, axis=-1, keepdims=True)
        first = jnp.min(jnp.where(work == m, lane, float(n_exp)), axis=-1, keepdims=True)
        hit = lane == first
        picked = jnp.where(hit, 1.0, picked)
        work = jnp.where(hit, -jnp.inf, work)
        firsts.append(first)
        vals.append(m)
    tri = (lax.broadcasted_iota(jnp.int32, (tm, tm), 0) > lax.broadcasted_iota(jnp.int32, (tm, tm), 1))
    before = jnp.dot(jnp.where(tri, 1.0, 0.0).astype(BF16), picked.astype(BF16), preferred_element_type=F32)
    before = before + cnt_s[...]
    ex = [jnp.exp(v - vals[0]) for v in vals]
    denom = ex[0] + ex[1] + ex[2] + ex[3]
    out_lane = lax.broadcasted_iota(jnp.int32, idx_ref.shape, 1)
    idx_out = jnp.zeros(idx_ref.shape, F32)
    rank_out = jnp.zeros(idx_ref.shape, F32)
    prob_out = jnp.zeros(idx_ref.shape, F32)
    for k in range(TOP_K):
        rank_k = jnp.sum(jnp.where(lane == firsts[k], before, 0.0), axis=-1, keepdims=True)
        idx_out = jnp.where(out_lane == k, firsts[k], idx_out)
        rank_out = jnp.where(out_lane == k, rank_k, rank_out)
        prob_out = jnp.where(out_lane == k, ex[k] / denom, prob_out)
    idx_ref[...] = idx_out.astype(jnp.int32)
    rank_ref[...] = rank_out.astype(jnp.int32)
    prob_ref[...] = prob_out
    cnt_s[...] = cnt_s[...] + jnp.sum(picked, axis=0, keepdims=True)
    cnt_ref[...] = cnt_s[...]


def _mod_specs(tm, t_len, n_batch, d, k_scale, k_shift):
    def lat(k):
        return pl.BlockSpec((None, None, 1, d), lambda i, *_: ((i * tm) // t_len, k, 0, 0))

    def ctx(k):
        return pl.BlockSpec((None, None, 1, d), lambda i, *_: (n_batch, k, 0, 0))
    return [lat(k_scale), lat(k_shift), ctx(k_scale), ctx(k_shift)]


def _norm_mod(z, g, mod, k_shift, k_scale, dims):
    n_batch, t_len, c_len = dims
    n, d = z.shape
    tm = _row_tile(t_len, 272)
    row_spec = pl.BlockSpec((tm, d), lambda i: (i, 0))
    return pl.pallas_call(
        functools.partial(_norm_mod_kernel, tm=tm, t_len=t_len, c_len=c_len),
        out_shape=jax.ShapeDtypeStruct((n, d), BF16),
        grid=(n // tm,),
        in_specs=[row_spec, pl.BlockSpec((1, d), lambda i: (0, 0))] + _mod_specs(tm, t_len, n_batch, d, k_scale, k_shift),
        out_specs=row_spec,
        compiler_params=_cparams("parallel"), name="norm_mod",
    )(z, g.reshape(1, d), mod, mod, mod, mod)


def _norm_router(z, g, mod, k_shift, k_scale, dims, rw, rb):
    n_batch, t_len, c_len = dims
    n, d = z.shape
    n_exp = rw.shape[1]
    tm = max(t for t in (LANES, 2 * LANES) if t_len % t == 0)
    row_spec = pl.BlockSpec((tm, d), lambda i: (i, 0))
    lane_spec = pl.BlockSpec((tm, LANES), lambda i: (i, 0))
    return pl.pallas_call(
        functools.partial(_norm_router_kernel, tm=tm, t_len=t_len, c_len=c_len),
        out_shape=(jax.ShapeDtypeStruct((n, d // 2), jnp.uint32), jax.ShapeDtypeStruct((n, LANES), jnp.int32),
                   jax.ShapeDtypeStruct((n, LANES), jnp.int32), jax.ShapeDtypeStruct((n, LANES), F32),
                   jax.ShapeDtypeStruct((1, n_exp), F32)),
        grid=(n // tm,),
        in_specs=[row_spec, pl.BlockSpec((1, d), lambda i: (0, 0))] + _mod_specs(tm, t_len, n_batch, d, k_scale, k_shift)
        + [pl.BlockSpec((d, n_exp), lambda i: (0, 0)), pl.BlockSpec((1, n_exp), lambda i: (0, 0))],
        out_specs=(pl.BlockSpec((tm, d // 2), lambda i: (i, 0)), lane_spec, lane_spec, lane_spec,
                   pl.BlockSpec((1, n_exp), lambda i: (0, 0))),
        scratch_shapes=[pltpu.VMEM((1, n_exp), F32)],
        compiler_params=_cparams("arbitrary"), name="norm_router",
    )(z, g.reshape(1, d), mod, mod, mod, mod, rw, rb.reshape(1, n_exp))


def _final_norm_kernel(z_ref, g_ref, o_ref):
    o_ref[...] = _normed(z_ref, g_ref)


def _final_norm(z, g, dims):
    n_batch, t_len, c_len = dims
    n, d = z.shape
    l_len = t_len - c_len
    tm = math.gcd(c_len, l_len)
    per_b = l_len // tm
    off = c_len // tm
    return pl.pallas_call(
        _final_norm_kernel,
        out_shape=jax.ShapeDtypeStruct((n_batch * l_len, d), F32),
        grid=(n_batch, per_b),
        in_specs=[pl.BlockSpec((tm, d), lambda b, s: (b * (t_len // tm) + off + s, 0)),
                  pl.BlockSpec((1, d), lambda b, s: (0, 0))],
        out_specs=pl.BlockSpec((tm, d), lambda b, s: (b * per_b + s, 0)),
        compiler_params=_cparams("parallel", "parallel"), name="final_norm",
    )(z, g.reshape(1, d)).reshape(n_batch, l_len, d)


def _mm_kernel(x_ref, w_ref, o_ref):
    o_ref[...] = jnp.dot(x_ref[...], w_ref[...], preferred_element_type=F32).astype(o_ref.dtype)


def _matmul(x, w, out_dtype, tm_target=1088, tn_target=512):
    n, k = x.shape
    n_out = w.shape[1]
    tm = _row_tile(n, tm_target)
    tn = _col_tile(n_out, tn_target)
    return pl.pallas_call(
        _mm_kernel,
        out_shape=jax.ShapeDtypeStruct((n, n_out), out_dtype),
        grid=(n // tm, n_out // tn),
        in_specs=[pl.BlockSpec((tm, k), lambda i, j: (i, 0)),
                  pl.BlockSpec((k, tn), lambda i, j: (0, j))],
        out_specs=pl.BlockSpec((tm, tn), lambda i, j: (i, j)),
        compiler_params=_cparams("parallel", "arbitrary"), name="matmul",
    )(x, w)


def _mm_residual_kernel(x_ref, w_ref, z_ref, gl_ref, gc_ref, o_ref, *, tm, t_len, c_len):
    acc = jnp.dot(x_ref[...], w_ref[...], preferred_element_type=F32)
    gate = jnp.where(_is_ctx_rows(pl.program_id(0), tm, t_len, c_len), gc_ref[...], gl_ref[...])
    o_ref[...] = z_ref[...] + gate * acc


def _matmul_residual(x, w, z, mod, k_gate, dims, tm_target=1088, tn_target=512):
    n_batch, t_len, c_len = dims
    n, k = x.shape
    d = w.shape[1]
    tm = _row_tile(t_len, tm_target)
    tn = _col_tile(d, tn_target)
    return pl.pallas_call(
        functools.partial(_mm_residual_kernel, tm=tm, t_len=t_len, c_len=c_len),
        out_shape=jax.ShapeDtypeStruct((n, d), F32),
        grid=(n // tm, d // tn),
        in_specs=[pl.BlockSpec((tm, k), lambda i, j: (i, 0)),
                  pl.BlockSpec((k, tn), lambda i, j: (0, j)),
                  pl.BlockSpec((tm, tn), lambda i, j: (i, j)),
                  pl.BlockSpec((None, None, 1, tn), lambda i, j: ((i * tm) // t_len, k_gate, 0, j)),
                  pl.BlockSpec((None, None, 1, tn), lambda i, j: (n_batch, k_gate, 0, j))],
        out_specs=pl.BlockSpec((tm, tn), lambda i, j: (i, j)),
        input_output_aliases={2: 0},
        compiler_params=_cparams("parallel", "arbitrary"), name="matmul_residual",
    )(x, w, z, mod, mod)


def _glu_kernel(x_ref, w_ref, b_ref, y_ref, o_ref):
    acc = jnp.dot(x_ref[...], w_ref[...], preferred_element_type=F32) + b_ref[...]
    o_ref[...] = (y_ref[...].astype(F32) * jax.nn.sigmoid(acc)).astype(o_ref.dtype)


def _glu(y, w, b):
    n, k = y.shape
    tm = _row_tile(n, 1088)
    tn = _col_tile(k, 512)
    return pl.pallas_call(
        _glu_kernel,
        out_shape=jax.ShapeDtypeStruct((n, k), BF16),
        grid=(n // tm, k // tn),
        in_specs=[pl.BlockSpec((tm, k), lambda i, j: (i, 0)),
                  pl.BlockSpec((k, tn), lambda i, j: (0, j)),
                  pl.BlockSpec((1, tn), lambda i, j: (0, j)),
                  pl.BlockSpec((tm, tn), lambda i, j: (i, j))],
        out_specs=pl.BlockSpec((tm, tn), lambda i, j: (i, j)),
        compiler_params=_cparams("parallel", "arbitrary"), name="s5_glu",
    )(y, w, b.reshape(1, k), y)


def _merge_kernel(h_ref, ya_ref, yb_ref, yc_ref, wg_ref, bg_ref, wb_ref, o_ref):
    h = h_ref[...]
    acc = None
    for b, y_ref in enumerate((ya_ref, yb_ref, yc_ref)):
        gate = jax.nn.sigmoid(jnp.dot(h, wg_ref[b], preferred_element_type=F32) + bg_ref[b])
        term = gate * jnp.dot(y_ref[...], wb_ref[b], preferred_element_type=F32)
        acc = term if acc is None else acc + term
    o_ref[...] = acc.astype(o_ref.dtype)


def _merge(h, ya, yb, yc, w_gate, b_gate, w_branch, tm_target=544, tn_target=256):
    n, d = h.shape
    w = ya.shape[1]
    tm = _row_tile(n, tm_target)
    tn = _col_tile(d, tn_target)
    y_spec = pl.BlockSpec((tm, w), lambda i, j: (i, 0))
    return pl.pallas_call(
        _merge_kernel,
        out_shape=jax.ShapeDtypeStruct((n, d), BF16),
        grid=(n // tm, d // tn),
        in_specs=[pl.BlockSpec((tm, d), lambda i, j: (i, 0)), y_spec, y_spec, y_spec,
                  pl.BlockSpec((N_BRANCH, d, tn), lambda i, j: (0, 0, j)),
                  pl.BlockSpec((N_BRANCH, 1, tn), lambda i, j: (0, 0, j)),
                  pl.BlockSpec((N_BRANCH, w, tn), lambda i, j: (0, 0, j))],
        out_specs=pl.BlockSpec((tm, tn), lambda i, j: (i, j)),
        compiler_params=_cparams("parallel", "arbitrary"), name="merge",
    )(h, ya, yb, yc, w_gate, b_gate.reshape(N_BRANCH, 1, d), w_branch)


def _static_chunks(start, size, step):
    return [(s, min(step, start + size - s)) for s in range(start, start + size, step)]


def _scan_block_real(a, b, row, reverse):
    for s in (1, 2, 4):
        if reverse:
            keep = row < SUBLANES - s
            shift = SUBLANES - s
        else:
            keep = row >= s
            shift = s
        a_prev = jnp.where(keep, pltpu.roll(a, shift, 0), 1.0)
        b_prev = jnp.where(keep, pltpu.roll(b, shift, 0), 0.0)
        b = a * b_prev + b
        a = a * a_prev
    return a, b


LRU_BLOCKS_PER_STEP = 2


def _lru_kernel(x_ref, g_ref, cw_ref, cb_ref, wa_ref, ba_ref, wx_ref, bx_ref, lam_ref, o_ref,
                xp_s, xc_s, a_s, b_s, *, c_len, l_len, rb):
    t_len = c_len + l_len
    width = x_ref.shape[1]
    bw = wa_ref.shape[-1]
    cw = cw_ref[...]
    cb = cb_ref[...]
    zeros8 = jnp.zeros((SUBLANES, width), F32)
    for seg0, seg_n in ((0, c_len), (c_len, l_len)):
        xp_s[0:SUBLANES, :] = zeros8
        xp_s[SUBLANES + seg_n:2 * SUBLANES + seg_n, :] = zeros8
        for s0, sn in _static_chunks(0, seg_n, rb):
            xp_s[SUBLANES + s0:SUBLANES + s0 + sn, :] = x_ref[seg0 + s0:seg0 + s0 + sn, :].astype(F32)
        for s0, sn in _static_chunks(0, seg_n, rb):
            acc = cb + xp_s[SUBLANES - 1 + s0:SUBLANES - 1 + s0 + sn, :] * cw[0:1]
            for j in range(1, CONV_W):
                acc = acc + xp_s[SUBLANES - 1 + j + s0:SUBLANES - 1 + j + s0 + sn, :] * cw[j:j + 1]
            xc_s[seg0 + s0:seg0 + s0 + sn, :] = acc

    for d in range(2):
        for k in range(width // bw):
            cols = slice(k * bw, (k + 1) * bw)
            wa = wa_ref[d, k]
            wx = wx_ref[d, k]
            ba = ba_ref[d:d + 1, cols]
            bx = bx_ref[d:d + 1, cols]
            sp = _softplus(-lam_ref[d:d + 1, cols])

            def gates(r, carry, d=d, cols=cols, wa=wa, wx=wx, ba=ba, bx=bx, sp=sp):
                rows = pl.ds(pl.multiple_of(r * rb, rb), rb)
                xc = xc_s[rows, cols]
                xb = xc.astype(BF16)
                rr = jax.nn.sigmoid(jnp.dot(xb, wa, preferred_element_type=F32) + ba)
                ii = jax.nn.sigmoid(jnp.dot(xb, wx, preferred_element_type=F32) + bx)
                a = jnp.exp((-LRU_C) * rr * sp)
                a_s[d, rows, cols] = a
                b_s[d, rows, cols] = jnp.sqrt(1.0 - a * a) * (ii * xc)
                return carry
            lax.fori_loop(0, t_len // rb, gates, 0)

    row = lax.broadcasted_iota(jnp.int32, (SUBLANES, width), 0)
    n_blk = t_len // SUBLANES
    c_blk = c_len // SUBLANES

    def scan(i, carry):
        h_f, h_b = carry
        blk_b = jnp.where(i < c_blk, c_blk - 1 - i, n_blk + c_blk - 1 - i)
        rows_f = pl.ds(pl.multiple_of(i * SUBLANES, SUBLANES), SUBLANES)
        rows_b = pl.ds(pl.multiple_of(blk_b * SUBLANES, SUBLANES), SUBLANES)
        af, bf = _scan_block_real(a_s[0, rows_f, :], b_s[0, rows_f, :], row, False)
        ab, bb = _scan_block_real(a_s[1, rows_b, :], b_s[1, rows_b, :], row, True)
        hs_f = af * h_f + bf
        hs_b = ab * h_b + bb
        a_s[0, rows_f, :] = hs_f
        a_s[1, rows_b, :] = hs_b
        return hs_f[SUBLANES - 1:SUBLANES, :], hs_b[0:1, :]
    h0 = jnp.zeros((1, width), F32)
    lax.fori_loop(0, n_blk, scan, (h0, h0))

    def finish(r, carry):
        rows = pl.ds(pl.multiple_of(r * rb, rb), rb)
        y = a_s[0, rows, :] + a_s[1, rows, :]
        o_ref[rows, :] = (jax.nn.gelu(g_ref[rows, :].astype(F32)) * y).astype(o_ref.dtype)
        return carry
    lax.fori_loop(0, t_len // rb, finish, 0)


def _lru_branch(u, conv_w, conv_b, w_a, b_a, w_x, b_x, lam, dims):
    n_batch, t_len, c_len = dims
    l_len = t_len - c_len
    n = u.shape[0]
    width = conv_w.shape[1]
    bw = width // LRU_BLOCKS
    per = LRU_BLOCKS_PER_STEP
    cw = per * bw
    rb = math.gcd(math.gcd(c_len, l_len), 256)
    vec2 = pl.BlockSpec((2, cw), lambda b, k: (0, k))
    wspec = pl.BlockSpec((2, per, bw, bw), lambda b, k: (0, k, 0, 0))
    return pl.pallas_call(
        functools.partial(_lru_kernel, c_len=c_len, l_len=l_len, rb=rb),
        out_shape=jax.ShapeDtypeStruct((n, width), BF16),
        grid=(n_batch, LRU_BLOCKS // per),
        in_specs=[pl.BlockSpec((t_len, cw), lambda b, k: (b, k)),
                  pl.BlockSpec((t_len, cw), lambda b, k: (b, LRU_BLOCKS // per + k)),
                  pl.BlockSpec((CONV_W, cw), lambda b, k: (0, k)),
                  pl.BlockSpec((1, cw), lambda b, k: (0, k)),
                  wspec, vec2, wspec, vec2, vec2],
        out_specs=pl.BlockSpec((t_len, cw), lambda b, k: (b, k)),
        scratch_shapes=[pltpu.VMEM((l_len + 2 * SUBLANES, cw), F32), pltpu.VMEM((t_len, cw), F32),
                        pltpu.VMEM((2, t_len, cw), F32), pltpu.VMEM((2, t_len, cw), F32)],
        compiler_params=_cparams("parallel", "parallel"), name="rglru",
    )(u, u, conv_w, conv_b.reshape(1, width), w_a.astype(BF16), b_a, w_x.astype(BF16), b_x, lam)


RET_HEADS_PER_STEP = 2


def _ret_kernel(q_ref, k_ref, v_ref, g_ref, cos_ref, sin_ref, dec_ref, o_ref,
                qs_s, ks_s, sb_s, dm_s, vec_s, *, c_len, l_len, rb):
    t_len = c_len + l_len
    hd = RET_HEAD_DIM
    heads = q_ref.shape[1] // hd
    ch = RET_CHUNK
    k_scale = hd ** -0.5
    ri = lax.broadcasted_iota(jnp.int32, (ch, ch), 0).astype(F32)
    ci = lax.broadcasted_iota(jnp.int32, (ch, ch), 1).astype(F32)
    diff = ri - ci
    pos = lax.broadcasted_iota(jnp.int32, (ch, hd), 0).astype(F32)
    cd_f, cd_b = [], []
    for h in range(heads):
        dec = dec_ref[:, h]
        lg_f = -_softplus(-dec[0])
        lg_b = -_softplus(-dec[1])
        dm_s[h] = (jnp.where(diff >= 0, jnp.exp(lg_f * jnp.maximum(diff, 0.0)), 0.0)
                   + jnp.where(diff <= 0, jnp.exp(lg_b * jnp.maximum(-diff, 0.0)), 0.0))
        vec_s[h, 0] = jnp.exp(lg_f * (ch - 1.0 - pos))
        vec_s[h, 1] = jnp.exp(lg_f * (pos + 1.0))
        vec_s[h, 2] = jnp.exp(lg_b * pos)
        vec_s[h, 3] = jnp.exp(lg_b * (ch - pos))
        cd_f.append(jnp.exp(lg_f * float(ch)))
        cd_b.append(jnp.exp(lg_b * float(ch)))

    def cols(h):
        return slice(h * hd, (h + 1) * hd)

    for s0, sn in _static_chunks(0, c_len, rb):
        qs_s[s0:s0 + sn, :] = q_ref[s0:s0 + sn, :].astype(F32)
        ks_s[s0:s0 + sn, :] = k_ref[s0:s0 + sn, :].astype(F32) * k_scale
    lane = lax.broadcasted_iota(jnp.int32, (rb, hd), 1)
    low = (lane % (hd // 2)) < (hd // 4)

    def rope(r, carry):
        src = pl.ds(pl.multiple_of(c_len + r * rb, rb), rb)
        tab = pl.ds(pl.multiple_of(r * rb, rb), rb)
        cs = cos_ref[tab, :]
        sn = sin_ref[tab, :]
        for ref, dst, scale in ((q_ref, qs_s, 1.0), (k_ref, ks_s, k_scale)):
            for h in range(heads):
                x = ref[src, cols(h)].astype(F32)
                partner = jnp.where(low, pltpu.roll(x, hd - hd // 4, 1), pltpu.roll(x, hd // 4, 1))
                y = x * cs + partner * sn
                dst[src, cols(h)] = y * scale if scale != 1.0 else y
        return carry
    lax.fori_loop(0, l_len // rb, rope, 0)

    def chunk_rows(c):
        return pl.ds(pl.multiple_of(c * ch, ch), ch)

    def kv_state(rows, h, kdec):
        kd = (ks_s[rows, cols(h)] * kdec).T.astype(BF16)
        return jnp.dot(kd, v_ref[rows, cols(h)], preferred_element_type=F32)

    n_ch = t_len // ch
    c_ch = c_len // ch

    def back(i, states, lo, hi):
        c = hi - 1 - i
        out = []
        for h in range(heads):
            sb_s[h, c] = states[h]
            out.append(cd_b[h] * states[h] + kv_state(chunk_rows(c), h, vec_s[h, 2]))
        return tuple(out)
    zero = tuple(jnp.zeros((hd, hd), F32) for _ in range(heads))
    s_ctx = lax.fori_loop(0, c_ch, functools.partial(back, lo=0, hi=c_ch), zero)
    lax.fori_loop(0, n_ch - c_ch, functools.partial(back, lo=c_ch, hi=n_ch), s_ctx)

    def fwd(c, states):
        rows = chunk_rows(c)
        out = []
        for h in range(heads):
            s = states[h]
            q = qs_s[rows, cols(h)]
            qb = q.astype(BF16)
            kb = ks_s[rows, cols(h)].astype(BF16)
            scores = lax.dot_general(qb, kb, (((1,), (1,)), ((), ())), preferred_element_type=F32) * dm_s[h]
            o = jnp.dot(scores.astype(BF16), v_ref[rows, cols(h)], preferred_element_type=F32)
            o = o + jnp.dot((q * vec_s[h, 1]).astype(BF16), s.astype(BF16), preferred_element_type=F32)
            o = o + jnp.dot((q * vec_s[h, 3]).astype(BF16), sb_s[h, c].astype(BF16), preferred_element_type=F32)
            o = o * lax.rsqrt(jnp.mean(o * o, axis=-1, keepdims=True) + NORM_EPS)
            g = g_ref[rows, cols(h)].astype(F32)
            o_ref[rows, cols(h)] = (o * (g * jax.nn.sigmoid(g))).astype(o_ref.dtype)
            out.append(cd_f[h] * s + kv_state(rows, h, vec_s[h, 0]))
        return tuple(out)
    lax.fori_loop(0, n_ch, fwd, zero)


def _rope_tables(l_len, hd):
    t = jnp.arange(l_len, dtype=jnp.int32)
    row = (t // GRID_W).astype(F32)
    col = (t % GRID_W).astype(F32)
    quarter = hd // 4
    freqs = ROPE_BASE ** (-jnp.arange(quarter, dtype=F32) / quarter)
    ang_r = row[:, None] * freqs
    ang_c = col[:, None] * freqs
    cos = jnp.concatenate([jnp.cos(ang_r)] * 2 + [jnp.cos(ang_c)] * 2, axis=-1)
    sin = jnp.concatenate([-jnp.sin(ang_r), jnp.sin(ang_r), -jnp.sin(ang_c), jnp.sin(ang_c)], axis=-1)
    return cos, sin


def _ret_branch(u, cos, sin, decay, dims):
    n_batch, t_len, c_len = dims
    l_len = t_len - c_len
    n = u.shape[0]
    hd = RET_HEAD_DIM
    n_heads = decay.shape[1]
    rb = math.gcd(math.gcd(c_len, l_len), 256)

    per = RET_HEADS_PER_STEP
    steps = n_heads // per
    wide = per * hd

    def col(split):
        return pl.BlockSpec((t_len, wide), lambda b, h: (b, split * steps + h))
    tab = pl.BlockSpec((l_len, hd), lambda b, h: (0, 0))
    return pl.pallas_call(
        functools.partial(_ret_kernel, c_len=c_len, l_len=l_len, rb=rb),
        out_shape=jax.ShapeDtypeStruct((n, n_heads * hd), BF16),
        grid=(n_batch, steps),
        in_specs=[col(2), col(3), col(4), col(5), tab, tab,
                  pl.BlockSpec((2, per, 1, 1), lambda b, h: (0, h, 0, 0))],
        out_specs=pl.BlockSpec((t_len, wide), lambda b, h: (b, h)),
        scratch_shapes=[pltpu.VMEM((t_len, wide), F32), pltpu.VMEM((t_len, wide), F32),
                        pltpu.VMEM((per, t_len // RET_CHUNK, hd, hd), F32),
                        pltpu.VMEM((per, RET_CHUNK, RET_CHUNK), F32),
                        pltpu.VMEM((per, 4, RET_CHUNK, hd), F32)],
        compiler_params=_cparams("parallel", "parallel"), name="retention",
    )(u, u, u, u, cos, sin, decay.reshape(2, n_heads, 1, 1))


def _s5_disc_kernel(are_ref, aim_ref, ldt_ref, bre_ref, bim_ref, pre_ref, pim_ref, bbre_ref, bbim_ref):
    lam_re = jnp.minimum(are_ref[...], -1e-4)
    lam_im = aim_ref[...]
    dt = jnp.exp(ldt_ref[...])
    z_re = lam_re * dt
    z_im = lam_im * dt
    mag = jnp.exp(z_re)
    ab_re = mag * jnp.cos(z_im)
    ab_im = mag * jnp.sin(z_im)
    den = lam_re * lam_re + lam_im * lam_im
    n_re = ab_re - 1.0
    co_re = (n_re * lam_re + ab_im * lam_im) / den
    co_im = (ab_im * lam_re - n_re * lam_im) / den
    for i in range(bre_ref.shape[0]):
        b_re = bre_ref[i]
        b_im = bim_ref[i]
        bbre_ref[i] = co_re * b_re - co_im * b_im
        bbim_ref[i] = co_re * b_im + co_im * b_re
    p_re, p_im = ab_re, ab_im
    pre_ref[0] = p_re
    pim_ref[0] = p_im
    for j in range(1, SUBLANES):
        p_re, p_im = p_re * ab_re - p_im * ab_im, p_re * ab_im + p_im * ab_re
        pre_ref[j] = p_re
        pim_ref[j] = p_im


def _s5_discretise(a_re, a_im, log_dt, b_re, b_im):
    two, g, p = a_re.shape
    i = b_re.shape[-1]
    rows = two * g
    full2 = pl.BlockSpec((rows, p), lambda: (0, 0))
    full3 = pl.BlockSpec((i, rows, p), lambda: (0, 0, 0))
    pw = pl.BlockSpec((SUBLANES, rows, p), lambda: (0, 0, 0))

    def input_major(m):
        return jnp.transpose(m, (3, 0, 1, 2)).reshape(i, rows, p)
    pw_re, pw_im, bb_re, bb_im = pl.pallas_call(
        _s5_disc_kernel,
        out_shape=(jax.ShapeDtypeStruct((SUBLANES, rows, p), F32), jax.ShapeDtypeStruct((SUBLANES, rows, p), F32),
                   jax.ShapeDtypeStruct((i, rows, p), F32), jax.ShapeDtypeStruct((i, rows, p), F32)),
        in_specs=[full2, full2, pl.BlockSpec((rows, 1), lambda: (0, 0)), full3, full3],
        out_specs=(pw, pw, full3, full3),
        name="s5_discretise",
    )(a_re.reshape(rows, p), a_im.reshape(rows, p), log_dt.reshape(rows, 1), input_major(b_re), input_major(b_im))
    return pw_re, pw_im, jnp.swapaxes(bb_re, 0, 1), jnp.swapaxes(bb_im, 0, 1)


def _s5_kernel(u_ref, bb_ref, cb_ref, pw_ref, dsk_ref, o_ref, hr_s, hi_s, y_s, lv_s, *, c_len, l_len, rb):
    t_len = c_len + l_len
    ns = hr_s.shape[1]
    row = lax.broadcasted_iota(jnp.int32, (SUBLANES, ns), 0)
    n_blk = t_len // SUBLANES
    c_blk = c_len // SUBLANES
    dsk = dsk_ref[...]

    def skip(r, carry):
        rows = pl.ds(pl.multiple_of(r * rb, rb), rb)
        y_s[rows, :] = dsk * u_ref[rows, :].astype(F32)
        return carry
    lax.fori_loop(0, t_len // rb, skip, 0)

    for d in range(2):
        reverse = d == 1
        bb = bb_ref[d]

        def drive(r, carry, bb=bb):
            rows = pl.ds(pl.multiple_of(r * rb, rb), rb)
            bu = jnp.dot(u_ref[rows, :].astype(BF16), bb, preferred_element_type=F32)
            hr_s[rows, :] = bu[:, :ns]
            hi_s[rows, :] = bu[:, ns:]
            return carry
        lax.fori_loop(0, t_len // rb, drive, 0)

        for lvl, s in enumerate((1, 2, 4)):
            keep = (row < SUBLANES - s) if reverse else (row >= s)
            for part in range(2):
                lv_s[lvl, part] = jnp.where(keep, pw_ref[d, part, s - 1:s, :], 0.0)
        for part in range(2):
            if reverse:
                for j in range(SUBLANES):
                    lv_s[3, part, j:j + 1, :] = pw_ref[d, part, SUBLANES - 1 - j:SUBLANES - j, :]
            else:
                lv_s[3, part] = pw_ref[d, part]

        def scan(i, carry, reverse=reverse, lo=0, hi=n_blk):
            cr, ci = carry
            blk = (hi - 1 - i) if reverse else (lo + i)
            rows = pl.ds(pl.multiple_of(blk * SUBLANES, SUBLANES), SUBLANES)
            hr = hr_s[rows, :]
            hi_ = hi_s[rows, :]
            for lvl, s in enumerate((1, 2, 4)):
                shift = SUBLANES - s if reverse else s
                ar = lv_s[lvl, 0]
                ai = lv_s[lvl, 1]
                sr = pltpu.roll(hr, shift, 0)
                si = pltpu.roll(hi_, shift, 0)
                hr, hi_ = hr + (ar * sr - ai * si), hi_ + (ar * si + ai * sr)
            pr = lv_s[3, 0]
            pi = lv_s[3, 1]
            hr, hi_ = hr + (pr * cr - pi * ci), hi_ + (pr * ci + pi * cr)
            hr_s[rows, :] = hr
            hi_s[rows, :] = hi_
            if reverse:
                return hr[0:1, :], hi_[0:1, :]
            return hr[SUBLANES - 1:SUBLANES, :], hi_[SUBLANES - 1:SUBLANES, :]

        zero = (jnp.zeros((1, ns), F32), jnp.zeros((1, ns), F32))
        if not reverse:
            lax.fori_loop(0, n_blk, scan, zero)
        else:
            mid = lax.fori_loop(0, c_blk, functools.partial(scan, lo=0, hi=c_blk), zero)
            lax.fori_loop(0, n_blk - c_blk, functools.partial(scan, lo=c_blk, hi=n_blk), mid)

        cb = cb_ref[d]

        def readout(r, carry, cb=cb):
            rows = pl.ds(pl.multiple_of(r * rb, rb), rb)
            y = jnp.dot(hr_s[rows, :].astype(BF16), cb[:ns, :], preferred_element_type=F32)
            y = y + jnp.dot(hi_s[rows, :].astype(BF16), cb[ns:, :], preferred_element_type=F32)
            y_s[rows, :] = y_s[rows, :] + y
            return carry
        lax.fori_loop(0, t_len // rb, readout, 0)

    def finish(r, carry):
        rows = pl.ds(pl.multiple_of(r * rb, rb), rb)
        o_ref[rows, :] = jax.nn.gelu(y_s[rows, :]).astype(o_ref.dtype)
        return carry
    lax.fori_loop(0, t_len // rb, finish, 0)


def _s5_branch(u, a_re, a_im, log_dt, b_re, b_im, c_re, c_im, d_skip, dims):
    n_batch, t_len, c_len = dims
    l_len = t_len - c_len
    n = u.shape[0]
    _, g, p = a_re.shape
    i = b_re.shape[-1]
    width = g * i
    gpb = LANES // i
    nb = g // gpb
    ns = gpb * p
    pw_re, pw_im, bb_re, bb_im = _s5_discretise(a_re, a_im, log_dt, b_re, b_im)
    eye = jnp.eye(gpb, dtype=F32)

    def blockdiag_in(m):
        m = m.reshape(2, nb, gpb, i, p)
        return jnp.einsum('dbgip,gh->dbgihp', m, eye).reshape(2, nb, gpb * i, ns)
    bb = jnp.concatenate([blockdiag_in(bb_re), blockdiag_in(bb_im)], axis=-1).astype(BF16)

    def blockdiag_out(m):
        m = m.reshape(2, nb, gpb, i, p)
        return jnp.einsum('dbgip,gh->dbgphi', m, eye).reshape(2, nb, ns, gpb * i)
    cb = jnp.concatenate([blockdiag_out(c_re), -blockdiag_out(c_im)], axis=-2).astype(BF16)
    pw = jnp.stack([pw_re.reshape(SUBLANES, 2, g * p), pw_im.reshape(SUBLANES, 2, g * p)], axis=0)
    pw = jnp.transpose(pw, (2, 0, 1, 3))
    rb = math.gcd(math.gcd(c_len, l_len), 256)
    cols = (N_IN_SPLITS - 1) * width // LANES
    return pl.pallas_call(
        functools.partial(_s5_kernel, c_len=c_len, l_len=l_len, rb=rb),
        out_shape=jax.ShapeDtypeStruct((n, width), BF16),
        grid=(n_batch, nb),
        in_specs=[pl.BlockSpec((t_len, LANES), lambda b, k: (b, cols + k)),
                  pl.BlockSpec((2, None, LANES, 2 * ns), lambda b, k: (0, k, 0, 0)),
                  pl.BlockSpec((2, None, 2 * ns, LANES), lambda b, k: (0, k, 0, 0)),
                  pl.BlockSpec((2, 2, SUBLANES, ns), lambda b, k: (0, 0, 0, k)),
                  pl.BlockSpec((1, LANES), lambda b, k: (0, k))],
        out_specs=pl.BlockSpec((t_len, LANES), lambda b, k: (b, k)),
        scratch_shapes=[pltpu.VMEM((t_len, ns), F32), pltpu.VMEM((t_len, ns), F32),
                        pltpu.VMEM((t_len, LANES), F32),
                        pltpu.VMEM((4, 2, SUBLANES, ns), F32)],
        compiler_params=_cparams("parallel", "parallel"), name="s5",
    )(u, bb, cb, pw, d_skip.reshape(1, width))


EXPERT_TILE = 256
GATHER_UNROLL = 8


def _expert_kernel(te_ref, nu_ref, tok_ref, hp_hbm, w1_ref, b1_ref, w2_ref, b2_ref, y_ref, xbuf, xb, sem,
                   *, tme, ff, dc):
    j = pl.program_id(0)
    n_used = nu_ref[0]
    slot = j % 2
    half = xbuf.shape[2]

    def row_copy(tile, r, s):
        tok = tok_ref[tile * tme + r]
        return pltpu.make_async_copy(hp_hbm.at[pl.ds(tok, 1), :], xbuf.at[s, pl.ds(r, 1), :], sem.at[s])

    def start_tile(tile, s):
        def body(r, c):
            row_copy(tile, r, s).start()
            return c
        lax.fori_loop(0, tme, body, 0, unroll=GATHER_UNROLL)

    @pl.when(j == 0)
    def _():
        start_tile(j, slot)

    def wait_tile(tile, s):
        def body(r, c):
            row_copy(tile, r, s).wait()
            return c
        lax.fori_loop(0, tme, body, 0, unroll=GATHER_UNROLL)

    @pl.when(j < n_used)
    def _():
        wait_tile(j, slot)
        rc = 2 * SUBLANES

        def unpack(q, c):
            rows = pl.ds(pl.multiple_of(q * rc, rc), rc)
            for c0 in range(0, half, COMBINE_COLS):
                lo, hi = _unpack_bf16_pairs(xbuf[slot, rows, c0:c0 + COMBINE_COLS])
                xb[rows, c0:c0 + COMBINE_COLS] = lo.astype(BF16)
                xb[rows, half + c0:half + c0 + COMBINE_COLS] = hi.astype(BF16)
            return c
        lax.fori_loop(0, tme // rc, unpack, 0)
        for r in range(tme):
            row_copy(j + 1, r, 1 - slot).start()
        hu = jnp.dot(xb[...], w1_ref[...], preferred_element_type=F32) + b1_ref[...]
        gate = jnp.minimum(hu[:, :ff], SWIGLU_LIMIT)
        up = jnp.clip(hu[:, ff:], -SWIGLU_LIMIT, SWIGLU_LIMIT)
        act = (gate * jax.nn.sigmoid(SWIGLU_ALPHA * gate) * (up + 1.0)).astype(BF16)
        for c0 in range(0, half, dc):
            lo = jnp.dot(act, w2_ref[:, c0:c0 + dc], preferred_element_type=F32) + b2_ref[:, c0:c0 + dc]
            hi = (jnp.dot(act, w2_ref[:, half + c0:half + c0 + dc], preferred_element_type=F32)
                  + b2_ref[:, half + c0:half + c0 + dc])
            y_ref[:, c0:c0 + dc] = _pack_halves(lo, hi)

    @pl.when(j == n_used)
    def _():
        wait_tile(j, slot)

    @pl.when(j >= n_used)
    def _():
        y_ref[...] = jnp.zeros(y_ref.shape, y_ref.dtype)


def _experts(hp, tile_expert, n_used, tok_of_slot, w1, b1, w2, b2):
    n_exp, d, ff2 = w1.shape
    half = d // 2
    n_tiles = tile_expert.shape[0]
    tme = EXPERT_TILE
    grid_spec = pltpu.PrefetchScalarGridSpec(
        num_scalar_prefetch=3,
        grid=(n_tiles,),
        in_specs=[pl.BlockSpec(memory_space=pl.ANY),
                  pl.BlockSpec((None, d, ff2), lambda j, te, nu, tok: (te[j], 0, 0)),
                  pl.BlockSpec((None, 1, ff2), lambda j, te, nu, tok: (te[j], 0, 0)),
                  pl.BlockSpec((None, ff2 // 2, d), lambda j, te, nu, tok: (te[j], 0, 0)),
                  pl.BlockSpec((None, 1, d), lambda j, te, nu, tok: (te[j], 0, 0))],
        out_specs=pl.BlockSpec((tme, half), lambda j, te, nu, tok: (j, 0)),
        scratch_shapes=[pltpu.VMEM((2, tme, half), jnp.uint32), pltpu.VMEM((tme, d), BF16),
                        pltpu.SemaphoreType.DMA((2,))],
    )
    return pl.pallas_call(
        functools.partial(_expert_kernel, tme=tme, ff=ff2 // 2, dc=_col_tile(half, 512)),
        out_shape=jax.ShapeDtypeStruct((n_tiles * tme, half), jnp.uint32),
        grid_spec=grid_spec,
        compiler_params=_cparams("arbitrary"), name="experts",
    )(tile_expert, n_used, tok_of_slot, hp, w1, b1.reshape(n_exp, 1, ff2), w2, b2.reshape(n_exp, 1, d))


COMBINE_ROWS = SUBLANES
COMBINE_COLS = 4 * LANES


def _combine_kernel(pos_ref, y_hbm, prob_ref, z_ref, gl_ref, gc_ref, o_ref, ybuf, sem, *, tm, t_len, c_len):
    i = pl.program_id(0)
    n_steps = pl.num_programs(0)
    slot = i % 2
    half = ybuf.shape[3]

    def row_copy(tile, r, k, s):
        row = pos_ref[(tile * tm + r) * TOP_K + k]
        return pltpu.make_async_copy(y_hbm.at[pl.ds(row, 1), :], ybuf.at[s, k, pl.ds(r, 1), :], sem.at[s])

    def start_tile(tile, s):
        def body(r, c):
            for k in range(TOP_K):
                row_copy(tile, r, k, s).start()
            return c
        lax.fori_loop(0, tm, body, 0, unroll=GATHER_UNROLL // TOP_K)

    @pl.when(i == 0)
    def _():
        start_tile(i, slot)

    def wait_tile(tile, s):
        def body(r, c):
            for k in range(TOP_K):
                row_copy(tile, r, k, s).wait()
            return c
        lax.fori_loop(0, tm, body, 0, unroll=GATHER_UNROLL // TOP_K)
    wait_tile(i, slot)

    rc = COMBINE_ROWS
    cw = COMBINE_COLS
    nxt = jnp.minimum(i + 1, n_steps - 1)

    def reduce_rows(q, c):
        for r in range(rc):
            for k in range(TOP_K):
                row_copy(nxt, q * rc + r, k, 1 - slot).start()
        rows = pl.ds(pl.multiple_of(q * rc, rc), rc)
        prob = prob_ref[rows, :]
        row = (i * tm) % t_len + q * rc + lax.broadcasted_iota(jnp.int32, (rc, 1), 0)
        is_ctx = row < c_len
        for c0 in range(0, half, cw):
            acc_lo = acc_hi = None
            for k in range(TOP_K):
                lo, hi = _unpack_bf16_pairs(ybuf[slot, k, rows, c0:c0 + cw])
                p = prob[:, k:k + 1]
                acc_lo = p * lo if acc_lo is None else acc_lo + p * lo
                acc_hi = p * hi if acc_hi is None else acc_hi + p * hi
            for acc, off in ((acc_lo, c0), (acc_hi, half + c0)):
                gate = jnp.where(is_ctx, gc_ref[:, off:off + cw], gl_ref[:, off:off + cw])
                o_ref[rows, off:off + cw] = z_ref[rows, off:off + cw] + gate * acc
        return c
    lax.fori_loop(0, tm // rc, reduce_rows, 0)

    @pl.when(i == n_steps - 1)
    def _():
        wait_tile(nxt, 1 - slot)


def _combine(y_sorted, pos, prob, z, mod, k_gate, dims):
    n_batch, t_len, c_len = dims
    n, d = z.shape
    half = d // 2
    tm = max(t for t in (LANES, 2 * LANES) if t_len % t == 0)
    grid_spec = pltpu.PrefetchScalarGridSpec(
        num_scalar_prefetch=1,
        grid=(n // tm,),
        in_specs=[pl.BlockSpec(memory_space=pl.ANY),
                  pl.BlockSpec((tm, LANES), lambda i, pos: (i, 0)),
                  pl.BlockSpec((tm, d), lambda i, pos: (i, 0)),
                  pl.BlockSpec((None, None, 1, d), lambda i, pos: ((i * tm) // t_len, k_gate, 0, 0)),
                  pl.BlockSpec((None, None, 1, d), lambda i, pos: (n_batch, k_gate, 0, 0))],
        out_specs=pl.BlockSpec((tm, d), lambda i, pos: (i, 0)),
        scratch_shapes=[pltpu.VMEM((2, TOP_K, tm, half), jnp.uint32), pltpu.SemaphoreType.DMA((2,))],
    )
    return pl.pallas_call(
        functools.partial(_combine_kernel, tm=tm, t_len=t_len, c_len=c_len),
        out_shape=jax.ShapeDtypeStruct((n, d), F32),
        grid_spec=grid_spec,
        input_output_aliases={3: 0},
        compiler_params=_cparams("arbitrary"), name="moe_combine",
    )(pos, y_sorted, prob, z, mod, mod)


def _route(idx, rank, counts, n_exp):
    n = idx.shape[0]
    tme = EXPERT_TILE
    n_tiles = (n * TOP_K + n_exp * (tme - 1)) // tme + 1
    counts = counts.reshape(n_exp).astype(jnp.int32)
    padded = ((counts + tme - 1) // tme) * tme
    ends = jnp.cumsum(padded)
    base = ends - padded
    idx = idx[:, :TOP_K]
    pos = jnp.take(base, idx) + rank[:, :TOP_K]
    tile_start = jnp.arange(n_tiles, dtype=jnp.int32) * tme
    tile_expert = jnp.sum((ends[None, :] <= tile_start[:, None]).astype(jnp.int32), axis=1)
    tile_expert = jnp.minimum(tile_expert, n_exp - 1)
    n_used = (ends[-1:] // tme).astype(jnp.int32)
    token = jnp.broadcast_to(jnp.arange(n, dtype=jnp.int32)[:, None], (n, TOP_K))
    tok_of_slot = jnp.zeros((n_tiles * tme,), jnp.int32).at[pos.reshape(-1)].set(
        token.reshape(-1), unique_indices=True, indices_are_sorted=False)
    return pos.reshape(-1).astype(jnp.int32), tile_expert, n_used, tok_of_slot


def _moe_sublayer(z, g, mod, dims, rw, rb, w1, b1, w2, b2):
    hp, idx, rank, prob, counts = _norm_router(z, g, mod, 3, 4, dims, rw, rb)
    pos, tile_expert, n_used, tok_of_slot = _route(idx, rank, counts, rw.shape[1])
    y_sorted = _experts(hp, tile_expert, n_used, tok_of_slot, w1, b1, w2, b2)
    return _combine(y_sorted, pos, prob, z, mod, 5, dims)


def _layer(z, p, cos, sin, dims):
    d = z.shape[1]
    mod = p['mod'].reshape(p['mod'].shape[0], N_MOD, 1, d)
    h = _norm_mod(z, p['norm_mix_g'], mod, 0, 1, dims)
    u = _matmul(h, p['w_in'].astype(BF16), BF16)
    ya = _lru_branch(u, p['conv_w'], p['conv_b'], p['lru_w_a'], p['lru_b_a'], p['lru_w_x'], p['lru_b_x'],
                     p['lru_lam'], dims)
    yb = _ret_branch(u, cos, sin, p['ret_decay'], dims)
    yc = _s5_branch(u, p['s5_a_re'], p['s5_a_im'], p['s5_log_dt'], p['s5_b_re'], p['s5_b_im'],
                    p['s5_c_re'], p['s5_c_im'], p['s5_d'], dims)
    yc = _glu(yc, p['s5_w_glu'].astype(BF16), p['s5_b_glu'])
    m = _merge(h, ya, yb, yc, p['w_gate'].astype(BF16), p['b_gate'], p['w_branch'].astype(BF16))
    z = _matmul_residual(m, p['w_out'].astype(BF16), z, mod, 2, dims)
    return _moe_sublayer(z, p['norm_ffn_g'], mod, dims, p['router_w'], p['router_b'],
                         p['moe_w1'].astype(BF16), p['moe_b1'], p['moe_w2'].astype(BF16), p['moe_b2'])


def kernel(x, c, ctx, c_ctx, mod_w_a, mod_w_b, mod_b, norm_mix_g, norm_ffn_g, w_in, conv_w, conv_b, lru_w_a, lru_b_a, lru_w_x, lru_b_x, lru_lam, ret_decay, s5_a_re, s5_a_im, s5_log_dt, s5_b_re, s5_b_im, s5_c_re, s5_c_im, s5_d, s5_w_glu, s5_b_glu, w_branch, w_gate, b_gate, w_out, router_w, router_b, moe_w1, moe_b1, moe_w2, moe_b2, final_norm_g):
    n_batch, l_len, d = x.shape
    c_len = ctx.shape[1]
    t_len = c_len + l_len
    dims = (n_batch, t_len, c_len)
    z = jnp.concatenate([ctx, x], axis=1).reshape(n_batch * t_len, d)
    pad = (-(n_batch + 1)) % SUBLANES
    cc = jnp.concatenate([c, c_ctx[None, :], jnp.zeros((pad, d), F32)], axis=0)
    mod = _modulation(cc, mod_w_a, mod_w_b, mod_b)
    cos, sin = _rope_tables(l_len, RET_HEAD_DIM)
    params = dict(mod=mod, norm_mix_g=norm_mix_g, norm_ffn_g=norm_ffn_g, w_in=w_in, conv_w=conv_w, conv_b=conv_b,
                  lru_w_a=lru_w_a, lru_b_a=lru_b_a, lru_w_x=lru_w_x, lru_b_x=lru_b_x, lru_lam=lru_lam,
                  ret_decay=ret_decay, s5_a_re=s5_a_re, s5_a_im=s5_a_im, s5_log_dt=s5_log_dt, s5_b_re=s5_b_re,
                  s5_b_im=s5_b_im, s5_c_re=s5_c_re, s5_c_im=s5_c_im, s5_d=s5_d, s5_w_glu=s5_w_glu,
                  s5_b_glu=s5_b_glu, w_branch=w_branch, w_gate=w_gate, b_gate=b_gate, w_out=w_out,
                  router_w=router_w, router_b=router_b, moe_w1=moe_w1, moe_b1=moe_b1, moe_w2=moe_w2,
                  moe_b2=moe_b2)

    def body(zc, p):
        return _layer(zc, p, cos, sin, dims), None
    z, _ = lax.scan(body, z, params)
    return _final_norm(z, final_norm_g, dims)
```

```python
import functools
import math

import jax
import jax.numpy as jnp
from jax import lax
from jax.experimental import pallas as pl
from jax.experimental.pallas import tpu as pltpu

F32 = jnp.float32
BF16 = jnp.bfloat16
HIGHEST = lax.Precision.HIGHEST

V7X_VMEM_BYTES = 64 * 1024 * 1024
VMEM_LIMIT = V7X_VMEM_BYTES - 8 * 1024 * 1024
SUBLANES = 8
LANES = 128

N_IN_SPLITS = 7
N_BRANCH = 3
N_MOD = 6
LRU_BLOCKS = 8
LRU_C = 8.0
CONV_W = 4
RET_HEAD_DIM = 128
RET_CHUNK = 128
ROPE_BASE = 10000.0
GRID_W = 64
S5_IN = 16
S5_STATE = 64
TOP_K = 4
SWIGLU_LIMIT = 7.0
SWIGLU_ALPHA = 1.702
NORM_EPS = 1e-6


def _cparams(*sem):
    return pltpu.CompilerParams(dimension_semantics=sem, vmem_limit_bytes=VMEM_LIMIT)


def _row_tile(t, target):
    best = None
    for d in range(16, min(t, target) + 1, 16):
        if t % d == 0:
            best = d
    assert best is not None, (t, target)
    return best


def _col_tile(n, target):
    best = None
    for d in range(LANES, min(n, target) + 1, LANES):
        if n % d == 0:
            best = d
    assert best is not None, (n, target)
    return best


def _softplus(x):
    return jnp.maximum(x, 0.0) + jnp.log1p(jnp.exp(-jnp.abs(x)))


def _is_ctx_rows(i, tm, t_len, c_len):
    row = (i * tm) % t_len + lax.broadcasted_iota(jnp.int32, (tm, 1), 0)
    return row < c_len


def _mod_kernel(cc_ref, wa_ref, wb_ref, b_ref, o_ref):
    cc = cc_ref[...]
    s = cc * jax.nn.sigmoid(cc)
    t = jnp.dot(s, wa_ref[...], precision=HIGHEST, preferred_element_type=F32)
    o_ref[...] = jnp.dot(t, wb_ref[...], precision=HIGHEST, preferred_element_type=F32) + b_ref[...]


def _modulation(cc, mod_w_a, mod_w_b, mod_b):
    depth, d, r = mod_w_a.shape
    rows = cc.shape[0]
    return pl.pallas_call(
        _mod_kernel,
        out_shape=jax.ShapeDtypeStruct((depth, rows, N_MOD * d), F32),
        grid=(depth, N_MOD),
        in_specs=[
            pl.BlockSpec((rows, d), lambda l, j: (0, 0)),
            pl.BlockSpec((None, d, r), lambda l, j: (l, 0, 0)),
            pl.BlockSpec((None, r, d), lambda l, j: (l, 0, j)),
            pl.BlockSpec((None, 1, d), lambda l, j: (l, 0, j)),
        ],
        out_specs=pl.BlockSpec((None, rows, d), lambda l, j: (l, 0, j)),
        compiler_params=_cparams("arbitrary", "arbitrary"),
        name="modulation",
    )(cc, mod_w_a, mod_w_b, mod_b.reshape(depth, 1, N_MOD * d))


def _normed(z_ref, g_ref):
    x = z_ref[...]
    return x * lax.rsqrt(jnp.mean(x * x, axis=-1, keepdims=True) + NORM_EPS) * g_ref[...]


def _norm_mod_kernel(z_ref, g_ref, scl_ref, shl_ref, scc_ref, shc_ref, o_ref, *, tm, t_len, c_len):
    y = _normed(z_ref, g_ref)
    is_ctx = _is_ctx_rows(pl.program_id(0), tm, t_len, c_len)
    scale = jnp.where(is_ctx, scc_ref[...], scl_ref[...])
    shift = jnp.where(is_ctx, shc_ref[...], shl_ref[...])
    o_ref[...] = (y * (1.0 + scale) + shift).astype(o_ref.dtype)


HIGH_HALF = 0xFFFF0000


def _pack_halves(lo, hi):
    lo_bits = pltpu.bitcast(lo.astype(BF16).astype(F32), jnp.uint32)
    hi_bits = pltpu.bitcast(hi.astype(BF16).astype(F32), jnp.uint32)
    return (lo_bits >> jnp.uint32(16)) | (hi_bits & jnp.uint32(HIGH_HALF))


def _pack_bf16_pairs(x):
    half = x.shape[1] // 2
    return _pack_halves(x[:, :half], x[:, half:])


def _unpack_bf16_pairs(w):
    return pltpu.bitcast(w << jnp.uint32(16), F32), pltpu.bitcast(w & jnp.uint32(HIGH_HALF), F32)


def _norm_router_kernel(z_ref, g_ref, scl_ref, shl_ref, scc_ref, shc_ref, rw_ref, rb_ref,
                        hp_ref, idx_ref, rank_ref, prob_ref, cnt_ref, cnt_s, *, tm, t_len, c_len):
    i = pl.program_id(0)

    @pl.when(i == 0)
    def _():
        cnt_s[...] = jnp.zeros(cnt_s.shape, F32)
    y = _normed(z_ref, g_ref)
    is_ctx = _is_ctx_rows(i, tm, t_len, c_len)
    scale = jnp.where(is_ctx, scc_ref[...], scl_ref[...])
    shift = jnp.where(is_ctx, shc_ref[...], shl_ref[...])
    h = y * (1.0 + scale) + shift
    hp_ref[...] = _pack_bf16_pairs(h)
    logits = jnp.dot(h, rw_ref[...], precision=HIGHEST, preferred_element_type=F32) + rb_ref[...]
    n_exp = logits.shape[1]
    lane = lax.broadcasted_iota(jnp.int32, logits.shape, 1).astype(F32)
    work = logits
    picked = jnp.zeros(logits.shape, F32)
    firsts, vals = [], []
    for k in range(TOP_K):
        m = jnp.max(work, axis=-1, keepdims=True)
        first = jnp.min(jnp.where(work == m, lane, float(n_exp)), axis=-1, keepdims=True)
        hit = lane == first
        picked = jnp.where(hit, 1.0, picked)
        work = jnp.where(hit, -jnp.inf, work)
        firsts.append(first)
        vals.append(m)
    tri = (lax.broadcasted_iota(jnp.int32, (tm, tm), 0) > lax.broadcasted_iota(jnp.int32, (tm, tm), 1))
    before = jnp.dot(jnp.where(tri, 1.0, 0.0).astype(BF16), picked.astype(BF16), preferred_element_type=F32)
    before = before + cnt_s[...]
    ex = [jnp.exp(v - vals[0]) for v in vals]
    denom = ex[0] + ex[1] + ex[2] + ex[3]
    out_lane = lax.broadcasted_iota(jnp.int32, idx_ref.shape, 1)
    idx_out = jnp.zeros(idx_ref.shape, F32)
    rank_out = jnp.zeros(idx_ref.shape, F32)
    prob_out = jnp.zeros(idx_ref.shape, F32)
    for k in range(TOP_K):
        rank_k = jnp.sum(jnp.where(lane == firsts[k], before, 0.0), axis=-1, keepdims=True)
        idx_out = jnp.where(out_lane == k, firsts[k], idx_out)
        rank_out = jnp.where(out_lane == k, rank_k, rank_out)
        prob_out = jnp.where(out_lane == k, ex[k] / denom, prob_out)
    idx_ref[...] = idx_out.astype(jnp.int32)
    rank_ref[...] = rank_out.astype(jnp.int32)
    prob_ref[...] = prob_out
    cnt_s[...] = cnt_s[...] + jnp.sum(picked, axis=0, keepdims=True)
    cnt_ref[...] = cnt_s[...]


def _mod_specs(tm, t_len, n_batch, d, k_scale, k_shift):
    def lat(k):
        return pl.BlockSpec((None, None, 1, d), lambda i, *_: ((i * tm) // t_len, k, 0, 0))

    def ctx(k):
        return pl.BlockSpec((None, None, 1, d), lambda i, *_: (n_batch, k, 0, 0))
    return [lat(k_scale), lat(k_shift), ctx(k_scale), ctx(k_shift)]


def _norm_mod(z, g, mod, k_shift, k_scale, dims):
    n_batch, t_len, c_len = dims
    n, d = z.shape
    tm = _row_tile(t_len, 272)
    row_spec = pl.BlockSpec((tm, d), lambda i: (i, 0))
    return pl.pallas_call(
        functools.partial(_norm_mod_kernel, tm=tm, t_len=t_len, c_len=c_len),
        out_shape=jax.ShapeDtypeStruct((n, d), BF16),
        grid=(n // tm,),
        in_specs=[row_spec, pl.BlockSpec((1, d), lambda i: (0, 0))] + _mod_specs(tm, t_len, n_batch, d, k_scale, k_shift),
        out_specs=row_spec,
        compiler_params=_cparams("parallel"), name="norm_mod",
    )(z, g.reshape(1, d), mod, mod, mod, mod)


def _norm_router(z, g, mod, k_shift, k_scale, dims, rw, rb):
    n_batch, t_len, c_len = dims
    n, d = z.shape
    n_exp = rw.shape[1]
    tm = max(t for t in (LANES, 2 * LANES) if t_len % t == 0)
    row_spec = pl.BlockSpec((tm, d), lambda i: (i, 0))
    lane_spec = pl.BlockSpec((tm, LANES), lambda i: (i, 0))
    return pl.pallas_call(
        functools.partial(_norm_router_kernel, tm=tm, t_len=t_len, c_len=c_len),
        out_shape=(jax.ShapeDtypeStruct((n, d // 2), jnp.uint32), jax.ShapeDtypeStruct((n, LANES), jnp.int32),
                   jax.ShapeDtypeStruct((n, LANES), jnp.int32), jax.ShapeDtypeStruct((n, LANES), F32),
                   jax.ShapeDtypeStruct((1, n_exp), F32)),
        grid=(n // tm,),
        in_specs=[row_spec, pl.BlockSpec((1, d), lambda i: (0, 0))] + _mod_specs(tm, t_len, n_batch, d, k_scale, k_shift)
        + [pl.BlockSpec((d, n_exp), lambda i: (0, 0)), pl.BlockSpec((1, n_exp), lambda i: (0, 0))],
        out_specs=(pl.BlockSpec((tm, d // 2), lambda i: (i, 0)), lane_spec, lane_spec, lane_spec,
                   pl.BlockSpec((1, n_exp), lambda i: (0, 0))),
        scratch_shapes=[pltpu.VMEM((1, n_exp), F32)],
        compiler_params=_cparams("arbitrary"), name="norm_router",
    )(z, g.reshape(1, d), mod, mod, mod, mod, rw, rb.reshape(1, n_exp))


def _final_norm_kernel(z_ref, g_ref, o_ref):
    o_ref[...] = _normed(z_ref, g_ref)


def _final_norm(z, g, dims):
    n_batch, t_len, c_len = dims
    n, d = z.shape
    l_len = t_len - c_len
    tm = math.gcd(c_len, l_len)
    per_b = l_len // tm
    off = c_len // tm
    return pl.pallas_call(
        _final_norm_kernel,
        out_shape=jax.ShapeDtypeStruct((n_batch * l_len, d), F32),
        grid=(n_batch, per_b),
        in_specs=[pl.BlockSpec((tm, d), lambda b, s: (b * (t_len // tm) + off + s, 0)),
                  pl.BlockSpec((1, d), lambda b, s: (0, 0))],
        out_specs=pl.BlockSpec((tm, d), lambda b, s: (b * per_b + s, 0)),
        compiler_params=_cparams("parallel", "parallel"), name="final_norm",
    )(z, g.reshape(1, d)).reshape(n_batch, l_len, d)


def _mm_kernel(x_ref, w_ref, o_ref):
    o_ref[...] = jnp.dot(x_ref[...], w_ref[...], preferred_element_type=F32).astype(o_ref.dtype)


def _matmul(x, w, out_dtype, tm_target=1088, tn_target=512):
    n, k = x.shape
    n_out = w.shape[1]
    tm = _row_tile(n, tm_target)
    tn = _col_tile(n_out, tn_target)
    return pl.pallas_call(
        _mm_kernel,
        out_shape=jax.ShapeDtypeStruct((n, n_out), out_dtype),
        grid=(n // tm, n_out // tn),
        in_specs=[pl.BlockSpec((tm, k), lambda i, j: (i, 0)),
                  pl.BlockSpec((k, tn), lambda i, j: (0, j))],
        out_specs=pl.BlockSpec((tm, tn), lambda i, j: (i, j)),
        compiler_params=_cparams("parallel", "arbitrary"), name="matmul",
    )(x, w)


def _mm_residual_kernel(x_ref, w_ref, z_ref, gl_ref, gc_ref, o_ref, *, tm, t_len, c_len):
    acc = jnp.dot(x_ref[...], w_ref[...], preferred_element_type=F32)
    gate = jnp.where(_is_ctx_rows(pl.program_id(0), tm, t_len, c_len), gc_ref[...], gl_ref[...])
    o_ref[...] = z_ref[...] + gate * acc


def _matmul_residual(x, w, z, mod, k_gate, dims, tm_target=1088, tn_target=512):
    n_batch, t_len, c_len = dims
    n, k = x.shape
    d = w.shape[1]
    tm = _row_tile(t_len, tm_target)
    tn = _col_tile(d, tn_target)
    return pl.pallas_call(
        functools.partial(_mm_residual_kernel, tm=tm, t_len=t_len, c_len=c_len),
        out_shape=jax.ShapeDtypeStruct((n, d), F32),
        grid=(n // tm, d // tn),
        in_specs=[pl.BlockSpec((tm, k), lambda i, j: (i, 0)),
                  pl.BlockSpec((k, tn), lambda i, j: (0, j)),
                  pl.BlockSpec((tm, tn), lambda i, j: (i, j)),
                  pl.BlockSpec((None, None, 1, tn), lambda i, j: ((i * tm) // t_len, k_gate, 0, j)),
                  pl.BlockSpec((None, None, 1, tn), lambda i, j: (n_batch, k_gate, 0, j))],
        out_specs=pl.BlockSpec((tm, tn), lambda i, j: (i, j)),
        input_output_aliases={2: 0},
        compiler_params=_cparams("parallel", "arbitrary"), name="matmul_residual",
    )(x, w, z, mod, mod)


def _glu_kernel(x_ref, w_ref, b_ref, y_ref, o_ref):
    acc = jnp.dot(x_ref[...], w_ref[...], preferred_element_type=F32) + b_ref[...]
    o_ref[...] = (y_ref[...].astype(F32) * jax.nn.sigmoid(acc)).astype(o_ref.dtype)


def _glu(y, w, b):
    n, k = y.shape
    tm = _row_tile(n, 1088)
    tn = _col_tile(k, 512)
    return pl.pallas_call(
        _glu_kernel,
        out_shape=jax.ShapeDtypeStruct((n, k), BF16),
        grid=(n // tm, k // tn),
        in_specs=[pl.BlockSpec((tm, k), lambda i, j: (i, 0)),
                  pl.BlockSpec((k, tn), lambda i, j: (0, j)),
                  pl.BlockSpec((1, tn), lambda i, j: (0, j)),
                  pl.BlockSpec((tm, tn), lambda i, j: (i, j))],
        out_specs=pl.BlockSpec((tm, tn), lambda i, j: (i, j)),
        compiler_params=_cparams("parallel", "arbitrary"), name="s5_glu",
    )(y, w, b.reshape(1, k), y)


def _merge_kernel(h_ref, ya_ref, yb_ref, yc_ref, wg_ref, bg_ref, wb_ref, o_ref):
    h = h_ref[...]
    acc = None
    for b, y_ref in enumerate((ya_ref, yb_ref, yc_ref)):
        gate = jax.nn.sigmoid(jnp.dot(h, wg_ref[b], preferred_element_type=F32) + bg_ref[b])
        term = gate * jnp.dot(y_ref[...], wb_ref[b], preferred_element_type=F32)
        acc = term if acc is None else acc + term
    o_ref[...] = acc.astype(o_ref.dtype)


def _merge(h, ya, yb, yc, w_gate, b_gate, w_branch, tm_target=544, tn_target=256):
    n, d = h.shape
    w = ya.shape[1]
    tm = _row_tile(n, tm_target)
    tn = _col_tile(d, tn_target)
    y_spec = pl.BlockSpec((tm, w), lambda i, j: (i, 0))
    return pl.pallas_call(
        _merge_kernel,
        out_shape=jax.ShapeDtypeStruct((n, d), BF16),
        grid=(n // tm, d // tn),
        in_specs=[pl.BlockSpec((tm, d), lambda i, j: (i, 0)), y_spec, y_spec, y_spec,
                  pl.BlockSpec((N_BRANCH, d, tn), lambda i, j: (0, 0, j)),
                  pl.BlockSpec((N_BRANCH, 1, tn), lambda i, j: (0, 0, j)),
                  pl.BlockSpec((N_BRANCH, w, tn), lambda i, j: (0, 0, j))],
        out_specs=pl.BlockSpec((tm, tn), lambda i, j: (i, j)),
        compiler_params=_cparams("parallel", "arbitrary"), name="merge",
    )(h, ya, yb, yc, w_gate, b_gate.reshape(N_BRANCH, 1, d), w_branch)


def _static_chunks(start, size, step):
    return [(s, min(step, start + size - s)) for s in range(start, start + size, step)]


def _scan_block_real(a, b, row, reverse):
    for s in (1, 2, 4):
        if reverse:
            keep = row < SUBLANES - s
            shift = SUBLANES - s
        else:
            keep = row >= s
            shift = s
        a_prev = jnp.where(keep, pltpu.roll(a, shift, 0), 1.0)
        b_prev = jnp.where(keep, pltpu.roll(b, shift, 0), 0.0)
        b = a * b_prev + b
        a = a * a_prev
    return a, b


LRU_BLOCKS_PER_STEP = 2


def _lru_kernel(x_ref, g_ref, cw_ref, cb_ref, wa_ref, ba_ref, wx_ref, bx_ref, lam_ref, o_ref,
                xp_s, xc_s, a_s, b_s, *, c_len, l_len, rb):
    t_len = c_len + l_len
    width = x_ref.shape[1]
    bw = wa_ref.shape[-1]
    cw = cw_ref[...]
    cb = cb_ref[...]
    zeros8 = jnp.zeros((SUBLANES, width), F32)
    for seg0, seg_n in ((0, c_len), (c_len, l_len)):
        xp_s[0:SUBLANES, :] = zeros8
        xp_s[SUBLANES + seg_n:2 * SUBLANES + seg_n, :] = zeros8
        for s0, sn in _static_chunks(0, seg_n, rb):
            xp_s[SUBLANES + s0:SUBLANES + s0 + sn, :] = x_ref[seg0 + s0:seg0 + s0 + sn, :].astype(F32)
        for s0, sn in _static_chunks(0, seg_n, rb):
            acc = cb + xp_s[SUBLANES - 1 + s0:SUBLANES - 1 + s0 + sn, :] * cw[0:1]
            for j in range(1, CONV_W):
                acc = acc + xp_s[SUBLANES - 1 + j + s0:SUBLANES - 1 + j + s0 + sn, :] * cw[j:j + 1]
            xc_s[seg0 + s0:seg0 + s0 + sn, :] = acc

    for d in range(2):
        for k in range(width // bw):
            cols = slice(k * bw, (k + 1) * bw)
            wa = wa_ref[d, k]
            wx = wx_ref[d, k]
            ba = ba_ref[d:d + 1, cols]
            bx = bx_ref[d:d + 1, cols]
            sp = _softplus(-lam_ref[d:d + 1, cols])

            def gates(r, carry, d=d, cols=cols, wa=wa, wx=wx, ba=ba, bx=bx, sp=sp):
                rows = pl.ds(pl.multiple_of(r * rb, rb), rb)
                xc = xc_s[rows, cols]
                xb = xc.astype(BF16)
                rr = jax.nn.sigmoid(jnp.dot(xb, wa, preferred_element_type=F32) + ba)
                ii = jax.nn.sigmoid(jnp.dot(xb, wx, preferred_element_type=F32) + bx)
                a = jnp.exp((-LRU_C) * rr * sp)
                a_s[d, rows, cols] = a
                b_s[d, rows, cols] = jnp.sqrt(1.0 - a * a) * (ii * xc)
                return carry
            lax.fori_loop(0, t_len // rb, gates, 0)

    row = lax.broadcasted_iota(jnp.int32, (SUBLANES, width), 0)
    n_blk = t_len // SUBLANES
    c_blk = c_len // SUBLANES

    def scan(i, carry):
        h_f, h_b = carry
        blk_b = jnp.where(i < c_blk, c_blk - 1 - i, n_blk + c_blk - 1 - i)
        rows_f = pl.ds(pl.multiple_of(i * SUBLANES, SUBLANES), SUBLANES)
        rows_b = pl.ds(pl.multiple_of(blk_b * SUBLANES, SUBLANES), SUBLANES)
        af, bf = _scan_block_real(a_s[0, rows_f, :], b_s[0, rows_f, :], row, False)
        ab, bb = _scan_block_real(a_s[1, rows_b, :], b_s[1, rows_b, :], row, True)
        hs_f = af * h_f + bf
        hs_b = ab * h_b + bb
        a_s[0, rows_f, :] = hs_f
        a_s[1, rows_b, :] = hs_b
        return hs_f[SUBLANES - 1:SUBLANES, :], hs_b[0:1, :]
    h0 = jnp.zeros((1, width), F32)
    lax.fori_loop(0, n_blk, scan, (h0, h0))

    def finish(r, carry):
        rows = pl.ds(pl.multiple_of(r * rb, rb), rb)
        y = a_s[0, rows, :] + a_s[1, rows, :]
        o_ref[rows, :] = (jax.nn.gelu(g_ref[rows, :].astype(F32)) * y).astype(o_ref.dtype)
        return carry
    lax.fori_loop(0, t_len // rb, finish, 0)


def _lru_branch(u, conv_w, conv_b, w_a, b_a, w_x, b_x, lam, dims):
    n_batch, t_len, c_len = dims
    l_len = t_len - c_len
    n = u.shape[0]
    width = conv_w.shape[1]
    bw = width // LRU_BLOCKS
    per = LRU_BLOCKS_PER_STEP
    cw = per * bw
    rb = math.gcd(math.gcd(c_len, l_len), 256)
    vec2 = pl.BlockSpec((2, cw), lambda b, k: (0, k))
    wspec = pl.BlockSpec((2, per, bw, bw), lambda b, k: (0, k, 0, 0))
    return pl.pallas_call(
        functools.partial(_lru_kernel, c_len=c_len, l_len=l_len, rb=rb),
        out_shape=jax.ShapeDtypeStruct((n, width), BF16),
        grid=(n_batch, LRU_BLOCKS // per),
        in_specs=[pl.BlockSpec((t_len, cw), lambda b, k: (b, k)),
                  pl.BlockSpec((t_len, cw), lambda b, k: (b, LRU_BLOCKS // per + k)),
                  pl.BlockSpec((CONV_W, cw), lambda b, k: (0, k)),
                  pl.BlockSpec((1, cw), lambda b, k: (0, k)),
                  wspec, vec2, wspec, vec2, vec2],
        out_specs=pl.BlockSpec((t_len, cw), lambda b, k: (b, k)),
        scratch_shapes=[pltpu.VMEM((l_len + 2 * SUBLANES, cw), F32), pltpu.VMEM((t_len, cw), F32),
                        pltpu.VMEM((2, t_len, cw), F32), pltpu.VMEM((2, t_len, cw), F32)],
        compiler_params=_cparams("parallel", "parallel"), name="rglru",
    )(u, u, conv_w, conv_b.reshape(1, width), w_a.astype(BF16), b_a, w_x.astype(BF16), b_x, lam)


RET_HEADS_PER_STEP = 2


def _ret_kernel(q_ref, k_ref, v_ref, g_ref, cos_ref, sin_ref, dec_ref, o_ref,
                qs_s, ks_s, sb_s, dm_s, vec_s, *, c_len, l_len, rb):
    t_len = c_len + l_len
    hd = RET_HEAD_DIM
    heads = q_ref.shape[1] // hd
    ch = RET_CHUNK
    k_scale = hd ** -0.5
    ri = lax.broadcasted_iota(jnp.int32, (ch, ch), 0).astype(F32)
    ci = lax.broadcasted_iota(jnp.int32, (ch, ch), 1).astype(F32)
    diff = ri - ci
    pos = lax.broadcasted_iota(jnp.int32, (ch, hd), 0).astype(F32)
    cd_f, cd_b = [], []
    for h in range(heads):
        dec = dec_ref[:, h]
        lg_f = -_softplus(-dec[0])
        lg_b = -_softplus(-dec[1])
        dm_s[h] = (jnp.where(diff >= 0, jnp.exp(lg_f * jnp.maximum(diff, 0.0)), 0.0)
                   + jnp.where(diff <= 0, jnp.exp(lg_b * jnp.maximum(-diff, 0.0)), 0.0))
        vec_s[h, 0] = jnp.exp(lg_f * (ch - 1.0 - pos))
        vec_s[h, 1] = jnp.exp(lg_f * (pos + 1.0))
        vec_s[h, 2] = jnp.exp(lg_b * pos)
        vec_s[h, 3] = jnp.exp(lg_b * (ch - pos))
        cd_f.append(jnp.exp(lg_f * float(ch)))
        cd_b.append(jnp.exp(lg_b * float(ch)))

    def cols(h):
        return slice(h * hd, (h + 1) * hd)

    for s0, sn in _static_chunks(0, c_len, rb):
        qs_s[s0:s0 + sn, :] = q_ref[s0:s0 + sn, :].astype(F32)
        ks_s[s0:s0 + sn, :] = k_ref[s0:s0 + sn, :].astype(F32) * k_scale
    lane = lax.broadcasted_iota(jnp.int32, (rb, hd), 1)
    low = (lane % (hd // 2)) < (hd // 4)

    def rope(r, carry):
        src = pl.ds(pl.multiple_of(c_len + r * rb, rb), rb)
        tab = pl.ds(pl.multiple_of(r * rb, rb), rb)
        cs = cos_ref[tab, :]
        sn = sin_ref[tab, :]
        for ref, dst, scale in ((q_ref, qs_s, 1.0), (k_ref, ks_s, k_scale)):
            for h in range(heads):
                x = ref[src, cols(h)].astype(F32)
                partner = jnp.where(low, pltpu.roll(x, hd - hd // 4, 1), pltpu.roll(x, hd // 4, 1))
                y = x * cs + partner * sn
                dst[src, cols(h)] = y * scale if scale != 1.0 else y
        return carry
    lax.fori_loop(0, l_len // rb, rope, 0)

    def chunk_rows(c):
        return pl.ds(pl.multiple_of(c * ch, ch), ch)

    def kv_state(rows, h, kdec):
        kd = (ks_s[rows, cols(h)] * kdec).T.astype(BF16)
        return jnp.dot(kd, v_ref[rows, cols(h)], preferred_element_type=F32)

    n_ch = t_len // ch
    c_ch = c_len // ch

    def back(i, states, lo, hi):
        c = hi - 1 - i
        out = []
        for h in range(heads):
            sb_s[h, c] = states[h]
            out.append(cd_b[h] * states[h] + kv_state(chunk_rows(c), h, vec_s[h, 2]))
        return tuple(out)
    zero = tuple(jnp.zeros((hd, hd), F32) for _ in range(heads))
    s_ctx = lax.fori_loop(0, c_ch, functools.partial(back, lo=0, hi=c_ch), zero)
    lax.fori_loop(0, n_ch - c_ch, functools.partial(back, lo=c_ch, hi=n_ch), s_ctx)

    def fwd(c, states):
        rows = chunk_rows(c)
        out = []
        for h in range(heads):
            s = states[h]
            q = qs_s[rows, cols(h)]
            qb = q.astype(BF16)
            kb = ks_s[rows, cols(h)].astype(BF16)
            scores = lax.dot_general(qb, kb, (((1,), (1,)), ((), ())), preferred_element_type=F32) * dm_s[h]
            o = jnp.dot(scores.astype(BF16), v_ref[rows, cols(h)], preferred_element_type=F32)
            o = o + jnp.dot((q * vec_s[h, 1]).astype(BF16), s.astype(BF16), preferred_element_type=F32)
            o = o + jnp.dot((q * vec_s[h, 3]).astype(BF16), sb_s[h, c].astype(BF16), preferred_element_type=F32)
            o = o * lax.rsqrt(jnp.mean(o * o, axis=-1, keepdims=True) + NORM_EPS)
            g = g_ref[rows, cols(h)].astype(F32)
            o_ref[rows, cols(h)] = (o * (g * jax.nn.sigmoid(g))).astype(o_ref.dtype)
            out.append(cd_f[h] * s + kv_state(rows, h, vec_s[h, 0]))
        return tuple(out)
    lax.fori_loop(0, n_ch, fwd, zero)


def _rope_tables(l_len, hd):
    t = jnp.arange(l_len, dtype=jnp.int32)
    row = (t // GRID_W).astype(F32)
    col = (t % GRID_W).astype(F32)
    quarter = hd // 4
    freqs = ROPE_BASE ** (-jnp.arange(quarter, dtype=F32) / quarter)
    ang_r = row[:, None] * freqs
    ang_c = col[:, None] * freqs
    cos = jnp.concatenate([jnp.cos(ang_r)] * 2 + [jnp.cos(ang_c)] * 2, axis=-1)
    sin = jnp.concatenate([-jnp.sin(ang_r), jnp.sin(ang_r), -jnp.sin(ang_c), jnp.sin(ang_c)], axis=-1)
    return cos, sin


def _ret_branch(u, cos, sin, decay, dims):
    n_batch, t_len, c_len = dims
    l_len = t_len - c_len
    n = u.shape[0]
    hd = RET_HEAD_DIM
    n_heads = decay.shape[1]
    rb = math.gcd(math.gcd(c_len, l_len), 256)

    per = RET_HEADS_PER_STEP
    steps = n_heads // per
    wide = per * hd

    def col(split):
        return pl.BlockSpec((t_len, wide), lambda b, h: (b, split * steps + h))
    tab = pl.BlockSpec((l_len, hd), lambda b, h: (0, 0))
    return pl.pallas_call(
        functools.partial(_ret_kernel, c_len=c_len, l_len=l_len, rb=rb),
        out_shape=jax.ShapeDtypeStruct((n, n_heads * hd), BF16),
        grid=(n_batch, steps),
        in_specs=[col(2), col(3), col(4), col(5), tab, tab,
                  pl.BlockSpec((2, per, 1, 1), lambda b, h: (0, h, 0, 0))],
        out_specs=pl.BlockSpec((t_len, wide), lambda b, h: (b, h)),
        scratch_shapes=[pltpu.VMEM((t_len, wide), F32), pltpu.VMEM((t_len, wide), F32),
                        pltpu.VMEM((per, t_len // RET_CHUNK, hd, hd), F32),
                        pltpu.VMEM((per, RET_CHUNK, RET_CHUNK), F32),
                        pltpu.VMEM((per, 4, RET_CHUNK, hd), F32)],
        compiler_params=_cparams("parallel", "parallel"), name="retention",
    )(u, u, u, u, cos, sin, decay.reshape(2, n_heads, 1, 1))


def _s5_disc_kernel(are_ref, aim_ref, ldt_ref, bre_ref, bim_ref, pre_ref, pim_ref, bbre_ref, bbim_ref):
    lam_re = jnp.minimum(are_ref[...], -1e-4)
    lam_im = aim_ref[...]
    dt = jnp.exp(ldt_ref[...])
    z_re = lam_re * dt
    z_im = lam_im * dt
    mag = jnp.exp(z_re)
    ab_re = mag * jnp.cos(z_im)
    ab_im = mag * jnp.sin(z_im)
    den = lam_re * lam_re + lam_im * lam_im
    n_re = ab_re - 1.0
    co_re = (n_re * lam_re + ab_im * lam_im) / den
    co_im = (ab_im * lam_re - n_re * lam_im) / den
    for i in range(bre_ref.shape[0]):
        b_re = bre_ref[i]
        b_im = bim_ref[i]
        bbre_ref[i] = co_re * b_re - co_im * b_im
        bbim_ref[i] = co_re * b_im + co_im * b_re
    p_re, p_im = ab_re, ab_im
    pre_ref[0] = p_re
    pim_ref[0] = p_im
    for j in range(1, SUBLANES):
        p_re, p_im = p_re * ab_re - p_im * ab_im, p_re * ab_im + p_im * ab_re
        pre_ref[j] = p_re
        pim_ref[j] = p_im


def _s5_discretise(a_re, a_im, log_dt, b_re, b_im):
    two, g, p = a_re.shape
    i = b_re.shape[-1]
    rows = two * g
    full2 = pl.BlockSpec((rows, p), lambda: (0, 0))
    full3 = pl.BlockSpec((i, rows, p), lambda: (0, 0, 0))
    pw = pl.BlockSpec((SUBLANES, rows, p), lambda: (0, 0, 0))

    def input_major(m):
        return jnp.transpose(m, (3, 0, 1, 2)).reshape(i, rows, p)
    pw_re, pw_im, bb_re, bb_im = pl.pallas_call(
        _s5_disc_kernel,
        out_shape=(jax.ShapeDtypeStruct((SUBLANES, rows, p), F32), jax.ShapeDtypeStruct((SUBLANES, rows, p), F32),
                   jax.ShapeDtypeStruct((i, rows, p), F32), jax.ShapeDtypeStruct((i, rows, p), F32)),
        in_specs=[full2, full2, pl.BlockSpec((rows, 1), lambda: (0, 0)), full3, full3],
        out_specs=(pw, pw, full3, full3),
        name="s5_discretise",
    )(a_re.reshape(rows, p), a_im.reshape(rows, p), log_dt.reshape(rows, 1), input_major(b_re), input_major(b_im))
    return pw_re, pw_im, jnp.swapaxes(bb_re, 0, 1), jnp.swapaxes(bb_im, 0, 1)


def _s5_kernel(u_ref, bb_ref, cb_ref, pw_ref, dsk_ref, o_ref, buf0, buf1, buf2, y_s, lv_s, *, c_len, l_len, rb):
    t_len = c_len + l_len
    ns = buf0.shape[1] // 2
    bufs = (buf0, buf1, buf2)
    row = lax.broadcasted_iota(jnp.int32, (SUBLANES, ns), 0)
    dsk = dsk_ref[...]

    def skip(r, carry):
        rows = pl.ds(pl.multiple_of(r * rb, rb), rb)
        y_s[rows, :] = dsk * u_ref[rows, :].astype(F32)
        return carry
    lax.fori_loop(0, t_len // rb, skip, 0)

    y_s[pl.ds(t_len, rb), :] = jnp.zeros((rb, y_s.shape[1]), F32)
    buf2[...] = jnp.zeros(buf2.shape, F32)
    n_ch = t_len // rb
    c_ch = c_len // rb
    n_iter = (n_ch + 3) // 3

    def chunk_rows(c):
        return pl.ds(c * rb, rb) if isinstance(c, int) else pl.ds(pl.multiple_of(c * rb, rb), rb)

    for d in range(2):
        reverse = d == 1
        bb = bb_ref[d]
        cb = cb_ref[d]

        for lvl, s in enumerate((1, 2, 4)):
            keep = (row < SUBLANES - s) if reverse else (row >= s)
            for part in range(2):
                lv_s[lvl, part] = jnp.where(keep, pw_ref[d, part, s - 1:s, :], 0.0)
        for part in range(2):
            if reverse:
                for j in range(SUBLANES):
                    lv_s[3, part, j:j + 1, :] = pw_ref[d, part, SUBLANES - 1 - j:SUBLANES - j, :]
            else:
                lv_s[3, part] = pw_ref[d, part]

        def chunk_of(i, reverse=reverse):
            if not reverse:
                return i
            return jnp.where(i < c_ch, c_ch - 1 - i, n_ch + c_ch - 1 - i)

        def drive(i, buf, bb=bb):
            src = jnp.where(i < n_ch, chunk_of(i), 0)
            buf[...] = jnp.dot(u_ref[chunk_rows(src), :], bb, preferred_element_type=F32)

        def readout(i, buf, cb=cb):
            dst = jnp.where(jnp.logical_and(i >= 0, i < n_ch), chunk_of(i), n_ch)
            rows = chunk_rows(dst)
            y = jnp.dot(buf[:, :ns].astype(BF16), cb[:ns, :], preferred_element_type=F32)
            y = y + jnp.dot(buf[:, ns:].astype(BF16), cb[ns:, :], preferred_element_type=F32)
            y_s[rows, :] = y_s[rows, :] + y

        def scan_chunk(buf, carry, reverse=reverse):
            cr, ci = carry
            order = range(rb // SUBLANES - 1, -1, -1) if reverse else range(rb // SUBLANES)
            for blk in order:
                rows = slice(blk * SUBLANES, (blk + 1) * SUBLANES)
                hr = buf[rows, :ns]
                hi_ = buf[rows, ns:]
                for lvl, s in enumerate((1, 2, 4)):
                    shift = SUBLANES - s if reverse else s
                    ar = lv_s[lvl, 0]
                    ai = lv_s[lvl, 1]
                    sr = pltpu.roll(hr, shift, 0)
                    si = pltpu.roll(hi_, shift, 0)
                    hr, hi_ = hr + (ar * sr - ai * si), hi_ + (ar * si + ai * sr)
                pr = lv_s[3, 0]
                pi = lv_s[3, 1]
                hr, hi_ = hr + (pr * cr - pi * ci), hi_ + (pr * ci + pi * cr)
                buf[rows, :ns] = hr
                buf[rows, ns:] = hi_
                if reverse:
                    cr, ci = hr[0:1, :], hi_[0:1, :]
                else:
                    cr, ci = hr[SUBLANES - 1:SUBLANES, :], hi_[SUBLANES - 1:SUBLANES, :]
            return cr, ci

        drive(0, bufs[0])

        def step(k, carry):
            for ph in range(3):
                i = 3 * k + ph
                drive(i + 1, bufs[(ph + 1) % 3])
                carry = scan_chunk(bufs[ph], carry)
                readout(i - 1, bufs[(ph + 2) % 3])
            return carry
        zero = (jnp.zeros((1, ns), F32), jnp.zeros((1, ns), F32))
        lax.fori_loop(0, n_iter, step, zero)

    def finish(r, carry):
        rows = pl.ds(pl.multiple_of(r * rb, rb), rb)
        o_ref[rows, :] = jax.nn.gelu(y_s[rows, :]).astype(o_ref.dtype)
        return carry
    lax.fori_loop(0, t_len // rb, finish, 0)


def _s5_branch(u, a_re, a_im, log_dt, b_re, b_im, c_re, c_im, d_skip, dims):
    n_batch, t_len, c_len = dims
    l_len = t_len - c_len
    n = u.shape[0]
    _, g, p = a_re.shape
    i = b_re.shape[-1]
    width = g * i
    gpb = LANES // i
    nb = g // gpb
    ns = gpb * p
    pw_re, pw_im, bb_re, bb_im = _s5_discretise(a_re, a_im, log_dt, b_re, b_im)
    eye = jnp.eye(gpb, dtype=F32)

    def blockdiag_in(m):
        m = m.reshape(2, nb, gpb, i, p)
        return jnp.einsum('dbgip,gh->dbgihp', m, eye).reshape(2, nb, gpb * i, ns)
    bb = jnp.concatenate([blockdiag_in(bb_re), blockdiag_in(bb_im)], axis=-1).astype(BF16)

    def blockdiag_out(m):
        m = m.reshape(2, nb, gpb, i, p)
        return jnp.einsum('dbgip,gh->dbgphi', m, eye).reshape(2, nb, ns, gpb * i)
    cb = jnp.concatenate([blockdiag_out(c_re), -blockdiag_out(c_im)], axis=-2).astype(BF16)
    pw = jnp.stack([pw_re.reshape(SUBLANES, 2, g * p), pw_im.reshape(SUBLANES, 2, g * p)], axis=0)
    pw = jnp.transpose(pw, (2, 0, 1, 3))
    rb = math.gcd(math.gcd(c_len, l_len), 256)
    cols = (N_IN_SPLITS - 1) * width // LANES
    return pl.pallas_call(
        functools.partial(_s5_kernel, c_len=c_len, l_len=l_len, rb=rb),
        out_shape=jax.ShapeDtypeStruct((n, width), BF16),
        grid=(n_batch, nb),
        in_specs=[pl.BlockSpec((t_len, LANES), lambda b, k: (b, cols + k)),
                  pl.BlockSpec((2, None, LANES, 2 * ns), lambda b, k: (0, k, 0, 0)),
                  pl.BlockSpec((2, None, 2 * ns, LANES), lambda b, k: (0, k, 0, 0)),
                  pl.BlockSpec((2, 2, SUBLANES, ns), lambda b, k: (0, 0, 0, k)),
                  pl.BlockSpec((1, LANES), lambda b, k: (0, k))],
        out_specs=pl.BlockSpec((t_len, LANES), lambda b, k: (b, k)),
        scratch_shapes=[pltpu.VMEM((rb, 2 * ns), F32), pltpu.VMEM((rb, 2 * ns), F32), pltpu.VMEM((rb, 2 * ns), F32),
                        pltpu.VMEM((t_len + rb, LANES), F32),
                        pltpu.VMEM((4, 2, SUBLANES, ns), F32)],
        compiler_params=_cparams("parallel", "parallel"), name="s5",
    )(u, bb, cb, pw, d_skip.reshape(1, width))


EXPERT_TILE = 256
EXPERT_LOOKAHEAD = 2
GATHER_UNROLL = 8


def _expert_kernel(te_ref, nu_ref, tok_ref, hp_hbm, w1_ref, b1_ref, w2_ref, b2_ref, y_ref, xbuf, xb, sem,
                   *, tme, ff, dc):
    j = pl.program_id(0)
    n_used = nu_ref[0]
    n_buf = xbuf.shape[0]
    ahead = n_buf - 1
    slot = j % n_buf
    half = xbuf.shape[2]

    def row_copy(tile, r):
        tok = tok_ref[tile * tme + r]
        s = tile % n_buf
        return pltpu.make_async_copy(hp_hbm.at[pl.ds(tok, 1), :], xbuf.at[s, pl.ds(r, 1), :], sem.at[s])

    def start_tile(tile):
        def body(r, c):
            row_copy(tile, r).start()
            return c
        lax.fori_loop(0, tme, body, 0, unroll=GATHER_UNROLL)

    @pl.when(j == 0)
    def _():
        for t in range(ahead):
            start_tile(t)

    def wait_tile(tile):
        def body(r, c):
            row_copy(tile, r).wait()
            return c
        lax.fori_loop(0, tme, body, 0, unroll=GATHER_UNROLL)

    @pl.when(j < n_used)
    def _():
        wait_tile(j)
        rc = 2 * SUBLANES

        def unpack(q, c):
            rows = pl.ds(pl.multiple_of(q * rc, rc), rc)
            for c0 in range(0, half, COMBINE_COLS):
                lo, hi = _unpack_bf16_pairs(xbuf[slot, rows, c0:c0 + COMBINE_COLS])
                xb[rows, c0:c0 + COMBINE_COLS] = lo.astype(BF16)
                xb[rows, half + c0:half + c0 + COMBINE_COLS] = hi.astype(BF16)
            return c
        lax.fori_loop(0, tme // rc, unpack, 0)
        for r in range(tme):
            row_copy(j + ahead, r).start()
        hu = jnp.dot(xb[...], w1_ref[...], preferred_element_type=F32) + b1_ref[...]
        gate = jnp.minimum(hu[:, :ff], SWIGLU_LIMIT)
        up = jnp.clip(hu[:, ff:], -SWIGLU_LIMIT, SWIGLU_LIMIT)
        act = (gate * jax.nn.sigmoid(SWIGLU_ALPHA * gate) * (up + 1.0)).astype(BF16)
        for c0 in range(0, half, dc):
            lo = jnp.dot(act, w2_ref[:, c0:c0 + dc], preferred_element_type=F32) + b2_ref[:, c0:c0 + dc]
            hi = (jnp.dot(act, w2_ref[:, half + c0:half + c0 + dc], preferred_element_type=F32)
                  + b2_ref[:, half + c0:half + c0 + dc])
            y_ref[:, c0:c0 + dc] = _pack_halves(lo, hi)

    @pl.when(jnp.logical_and(j >= n_used, j < n_used + ahead))
    def _():
        wait_tile(j)

    @pl.when(j >= n_used)
    def _():
        y_ref[...] = jnp.zeros(y_ref.shape, y_ref.dtype)


def _experts(hp, tile_expert, n_used, tok_of_slot, w1, b1, w2, b2):
    n_exp, d, ff2 = w1.shape
    half = d // 2
    n_tiles = tile_expert.shape[0]
    tme = EXPERT_TILE
    grid_spec = pltpu.PrefetchScalarGridSpec(
        num_scalar_prefetch=3,
        grid=(n_tiles,),
        in_specs=[pl.BlockSpec(memory_space=pl.ANY),
                  pl.BlockSpec((None, d, ff2), lambda j, te, nu, tok: (te[j], 0, 0)),
                  pl.BlockSpec((None, 1, ff2), lambda j, te, nu, tok: (te[j], 0, 0)),
                  pl.BlockSpec((None, ff2 // 2, d), lambda j, te, nu, tok: (te[j], 0, 0)),
                  pl.BlockSpec((None, 1, d), lambda j, te, nu, tok: (te[j], 0, 0))],
        out_specs=pl.BlockSpec((tme, half), lambda j, te, nu, tok: (j, 0)),
        scratch_shapes=[pltpu.VMEM((EXPERT_LOOKAHEAD + 1, tme, half), jnp.uint32), pltpu.VMEM((tme, d), BF16),
                        pltpu.SemaphoreType.DMA((EXPERT_LOOKAHEAD + 1,))],
    )
    return pl.pallas_call(
        functools.partial(_expert_kernel, tme=tme, ff=ff2 // 2, dc=_col_tile(half, 512)),
        out_shape=jax.ShapeDtypeStruct((n_tiles * tme, half), jnp.uint32),
        grid_spec=grid_spec,
        compiler_params=_cparams("arbitrary"), name="experts",
    )(tile_expert, n_used, tok_of_slot, hp, w1, b1.reshape(n_exp, 1, ff2), w2, b2.reshape(n_exp, 1, d))


COMBINE_ROWS = SUBLANES
COMBINE_COLS = 4 * LANES


def _combine_kernel(pos_ref, y_hbm, prob_ref, z_ref, gl_ref, gc_ref, o_ref, ybuf, sem, *, tm, t_len, c_len):
    i = pl.program_id(0)
    n_steps = pl.num_programs(0)
    slot = i % 2
    half = ybuf.shape[3]

    def row_copy(tile, r, k, s):
        row = pos_ref[(tile * tm + r) * TOP_K + k]
        return pltpu.make_async_copy(y_hbm.at[pl.ds(row, 1), :], ybuf.at[s, k, pl.ds(r, 1), :], sem.at[s])

    def start_tile(tile, s):
        def body(r, c):
            for k in range(TOP_K):
                row_copy(tile, r, k, s).start()
            return c
        lax.fori_loop(0, tm, body, 0, unroll=GATHER_UNROLL // TOP_K)

    @pl.when(i == 0)
    def _():
        start_tile(i, slot)

    def wait_tile(tile, s):
        def body(r, c):
            for k in range(TOP_K):
                row_copy(tile, r, k, s).wait()
            return c
        lax.fori_loop(0, tm, body, 0, unroll=GATHER_UNROLL // TOP_K)
    wait_tile(i, slot)

    rc = COMBINE_ROWS
    cw = COMBINE_COLS
    nxt = jnp.minimum(i + 1, n_steps - 1)

    def reduce_rows(q, c):
        for r in range(rc):
            for k in range(TOP_K):
                row_copy(nxt, q * rc + r, k, 1 - slot).start()
        rows = pl.ds(q * rc, rc)
        prob = prob_ref[rows, :]
        row = (i * tm) % t_len + q * rc + lax.broadcasted_iota(jnp.int32, (rc, 1), 0)
        is_ctx = row < c_len
        for c0 in range(0, half, cw):
            acc_lo = acc_hi = None
            for k in range(TOP_K):
                lo, hi = _unpack_bf16_pairs(ybuf[slot, k, rows, c0:c0 + cw])
                p = prob[:, k:k + 1]
                acc_lo = p * lo if acc_lo is None else acc_lo + p * lo
                acc_hi = p * hi if acc_hi is None else acc_hi + p * hi
            for acc, off in ((acc_lo, c0), (acc_hi, half + c0)):
                gate = jnp.where(is_ctx, gc_ref[:, off:off + cw], gl_ref[:, off:off + cw])
                o_ref[rows, off:off + cw] = z_ref[rows, off:off + cw] + gate * acc
        return c
    for q in range(tm // rc):
        reduce_rows(q, 0)

    @pl.when(i == n_steps - 1)
    def _():
        wait_tile(nxt, 1 - slot)


def _combine(y_sorted, pos, prob, z, mod, k_gate, dims):
    n_batch, t_len, c_len = dims
    n, d = z.shape
    half = d // 2
    tm = max(t for t in (LANES, 2 * LANES) if t_len % t == 0)
    grid_spec = pltpu.PrefetchScalarGridSpec(
        num_scalar_prefetch=1,
        grid=(n // tm,),
        in_specs=[pl.BlockSpec(memory_space=pl.ANY),
                  pl.BlockSpec((tm, LANES), lambda i, pos: (i, 0)),
                  pl.BlockSpec((tm, d), lambda i, pos: (i, 0)),
                  pl.BlockSpec((None, None, 1, d), lambda i, pos: ((i * tm) // t_len, k_gate, 0, 0)),
                  pl.BlockSpec((None, None, 1, d), lambda i, pos: (n_batch, k_gate, 0, 0))],
        out_specs=pl.BlockSpec((tm, d), lambda i, pos: (i, 0)),
        scratch_shapes=[pltpu.VMEM((2, TOP_K, tm, half), jnp.uint32), pltpu.SemaphoreType.DMA((2,))],
    )
    return pl.pallas_call(
        functools.partial(_combine_kernel, tm=tm, t_len=t_len, c_len=c_len),
        out_shape=jax.ShapeDtypeStruct((n, d), F32),
        grid_spec=grid_spec,
        input_output_aliases={3: 0},
        compiler_params=_cparams("arbitrary"), name="moe_combine",
    )(pos, y_sorted, prob, z, mod, mod)


def _route(idx, rank, counts, n_exp):
    n = idx.shape[0]
    tme = EXPERT_TILE
    n_tiles = (n * TOP_K + n_exp * (tme - 1)) // tme + EXPERT_LOOKAHEAD
    counts = counts.reshape(n_exp).astype(jnp.int32)
    padded = ((counts + tme - 1) // tme) * tme
    ends = jnp.cumsum(padded)
    base = ends - padded
    idx = idx[:, :TOP_K]
    pos = jnp.take(base, idx) + rank[:, :TOP_K]
    tile_start = jnp.arange(n_tiles, dtype=jnp.int32) * tme
    tile_expert = jnp.sum((ends[None, :] <= tile_start[:, None]).astype(jnp.int32), axis=1)
    tile_expert = jnp.minimum(tile_expert, n_exp - 1)
    n_used = (ends[-1:] // tme).astype(jnp.int32)
    token = jnp.broadcast_to(jnp.arange(n, dtype=jnp.int32)[:, None], (n, TOP_K))
    tok_of_slot = jnp.zeros((n_tiles * tme,), jnp.int32).at[pos.reshape(-1)].set(
        token.reshape(-1), unique_indices=True, indices_are_sorted=False)
    return pos.reshape(-1).astype(jnp.int32), tile_expert, n_used, tok_of_slot


def _moe_sublayer(z, g, mod, dims, rw, rb, w1, b1, w2, b2):
    hp, idx, rank, prob, counts = _norm_router(z, g, mod, 3, 4, dims, rw, rb)
    pos, tile_expert, n_used, tok_of_slot = _route(idx, rank, counts, rw.shape[1])
    y_sorted = _experts(hp, tile_expert, n_used, tok_of_slot, w1, b1, w2, b2)
    return _combine(y_sorted, pos, prob, z, mod, 5, dims)


def _layer(z, p, cos, sin, dims):
    d = z.shape[1]
    mod = p['mod'].reshape(p['mod'].shape[0], N_MOD, 1, d)
    h = _norm_mod(z, p['norm_mix_g'], mod, 0, 1, dims)
    u = _matmul(h, p['w_in'].astype(BF16), BF16)
    ya = _lru_branch(u, p['conv_w'], p['conv_b'], p['lru_w_a'], p['lru_b_a'], p['lru_w_x'], p['lru_b_x'],
                     p['lru_lam'], dims)
    yb = _ret_branch(u, cos, sin, p['ret_decay'], dims)
    yc = _s5_branch(u, p['s5_a_re'], p['s5_a_im'], p['s5_log_dt'], p['s5_b_re'], p['s5_b_im'],
                    p['s5_c_re'], p['s5_c_im'], p['s5_d'], dims)
    yc = _glu(yc, p['s5_w_glu'].astype(BF16), p['s5_b_glu'])
    m = _merge(h, ya, yb, yc, p['w_gate'].astype(BF16), p['b_gate'], p['w_branch'].astype(BF16))
    z = _matmul_residual(m, p['w_out'].astype(BF16), z, mod, 2, dims)
    return _moe_sublayer(z, p['norm_ffn_g'], mod, dims, p['router_w'], p['router_b'],
                         p['moe_w1'].astype(BF16), p['moe_b1'], p['moe_w2'].astype(BF16), p['moe_b2'])


def kernel(x, c, ctx, c_ctx, mod_w_a, mod_w_b, mod_b, norm_mix_g, norm_ffn_g, w_in, conv_w, conv_b, lru_w_a, lru_b_a, lru_w_x, lru_b_x, lru_lam, ret_decay, s5_a_re, s5_a_im, s5_log_dt, s5_b_re, s5_b_im, s5_c_re, s5_c_im, s5_d, s5_w_glu, s5_b_glu, w_branch, w_gate, b_gate, w_out, router_w, router_b, moe_w1, moe_b1, moe_w2, moe_b2, final_norm_g):
    n_batch, l_len, d = x.shape
    c_len = ctx.shape[1]
    t_len = c_len + l_len
    dims = (n_batch, t_len, c_len)
    z = jnp.concatenate([ctx, x], axis=1).reshape(n_batch * t_len, d)
    pad = (-(n_batch + 1)) % SUBLANES
    cc = jnp.concatenate([c, c_ctx[None, :], jnp.zeros((pad, d), F32)], axis=0)
    mod = _modulation(cc, mod_w_a, mod_w_b, mod_b)
    cos, sin = _rope_tables(l_len, RET_HEAD_DIM)
    params = dict(mod=mod, norm_mix_g=norm_mix_g, norm_ffn_g=norm_ffn_g, w_in=w_in, conv_w=conv_w, conv_b=conv_b,
                  lru_w_a=lru_w_a, lru_b_a=lru_b_a, lru_w_x=lru_w_x, lru_b_x=lru_b_x, lru_lam=lru_lam,
                  ret_decay=ret_decay, s5_a_re=s5_a_re, s5_a_im=s5_a_im, s5_log_dt=s5_log_dt, s5_b_re=s5_b_re,
                  s5_b_im=s5_b_im, s5_c_re=s5_c_re, s5_c_im=s5_c_im, s5_d=s5_d, s5_w_glu=s5_w_glu,
                  s5_b_glu=s5_b_glu, w_branch=w_branch, w_gate=w_gate, b_gate=b_gate, w_out=w_out,
                  router_w=router_w, router_b=router_b, moe_w1=moe_w1, moe_b1=moe_b1, moe_w2=moe_w2,
                  moe_b2=moe_b2)

    def body(zc, p):
        return _layer(zc, p, cos, sin, dims), None
    z, _ = lax.scan(body, z, params)
    return _final_norm(z, final_norm_g, dims)
```

```python
import functools
import math

import jax
import jax.numpy as jnp
from jax import lax
from jax.experimental import pallas as pl
from jax.experimental.pallas import tpu as pltpu

F32 = jnp.float32
BF16 = jnp.bfloat16
HIGHEST = lax.Precision.HIGHEST

V7X_VMEM_BYTES = 64 * 1024 * 1024
VMEM_LIMIT = V7X_VMEM_BYTES - 8 * 1024 * 1024
SUBLANES = 8
LANES = 128

N_IN_SPLITS = 7
N_BRANCH = 3
N_MOD = 6
LRU_BLOCKS = 8
LRU_C = 8.0
CONV_W = 4
RET_HEAD_DIM = 128
RET_CHUNK = 128
ROPE_BASE = 10000.0
GRID_W = 64
S5_IN = 16
S5_STATE = 64
TOP_K = 4
SWIGLU_LIMIT = 7.0
SWIGLU_ALPHA = 1.702
NORM_EPS = 1e-6


def _cparams(*sem):
    return pltpu.CompilerParams(dimension_semantics=sem, vmem_limit_bytes=VMEM_LIMIT)


def _row_tile(t, target):
    best = None
    for d in range(16, min(t, target) + 1, 16):
        if t % d == 0:
            best = d
    assert best is not None, (t, target)
    return best


def _col_tile(n, target):
    best = None
    for d in range(LANES, min(n, target) + 1, LANES):
        if n % d == 0:
            best = d
    assert best is not None, (n, target)
    return best


def _softplus(x):
    return jnp.maximum(x, 0.0) + jnp.log1p(jnp.exp(-jnp.abs(x)))


def _is_ctx_rows(i, tm, t_len, c_len):
    row = (i * tm) % t_len + lax.broadcasted_iota(jnp.int32, (tm, 1), 0)
    return row < c_len


def _mod_kernel(cc_ref, wa_ref, wb_ref, b_ref, o_ref):
    cc = cc_ref[...]
    s = cc * jax.nn.sigmoid(cc)
    t = jnp.dot(s, wa_ref[...], precision=HIGHEST, preferred_element_type=F32)
    o_ref[...] = jnp.dot(t, wb_ref[...], precision=HIGHEST, preferred_element_type=F32) + b_ref[...]


def _modulation(cc, mod_w_a, mod_w_b, mod_b):
    depth, d, r = mod_w_a.shape
    rows = cc.shape[0]
    return pl.pallas_call(
        _mod_kernel,
        out_shape=jax.ShapeDtypeStruct((depth, rows, N_MOD * d), F32),
        grid=(depth, N_MOD),
        in_specs=[
            pl.BlockSpec((rows, d), lambda l, j: (0, 0)),
            pl.BlockSpec((None, d, r), lambda l, j: (l, 0, 0)),
            pl.BlockSpec((None, r, d), lambda l, j: (l, 0, j)),
            pl.BlockSpec((None, 1, d), lambda l, j: (l, 0, j)),
        ],
        out_specs=pl.BlockSpec((None, rows, d), lambda l, j: (l, 0, j)),
        compiler_params=_cparams("arbitrary", "arbitrary"),
        name="modulation",
    )(cc, mod_w_a, mod_w_b, mod_b.reshape(depth, 1, N_MOD * d))


NORM_ROWS = 16
NORM_COLS = 4 * LANES


def _inv_rms(z_ref, rows):
    d = z_ref.shape[1]
    ss = None
    for c0 in range(0, d, NORM_COLS):
        x = z_ref[rows, c0:c0 + NORM_COLS]
        ss = x * x if ss is None else ss + x * x
    return lax.rsqrt(jnp.sum(ss, axis=-1, keepdims=True) / d + NORM_EPS)


def _for_row_blocks(tm, body):
    def step(q, carry):
        body(pl.ds(pl.multiple_of(q * NORM_ROWS, NORM_ROWS), NORM_ROWS), q * NORM_ROWS)
        return carry
    lax.fori_loop(0, tm // NORM_ROWS, step, 0)


def _modulated_chunk(z_ref, g_ref, mods, rows, inv, is_ctx, c0):
    scl_ref, shl_ref, scc_ref, shc_ref = mods
    cols = slice(c0, c0 + NORM_COLS)
    y = z_ref[rows, cols] * inv * g_ref[:, cols]
    scale = jnp.where(is_ctx, scc_ref[:, cols], scl_ref[:, cols])
    shift = jnp.where(is_ctx, shc_ref[:, cols], shl_ref[:, cols])
    return y * (1.0 + scale) + shift


def _ctx_rows_of_block(i, tm, t_len, c_len, first_row):
    row = (i * tm) % t_len + first_row + lax.broadcasted_iota(jnp.int32, (NORM_ROWS, 1), 0)
    return row < c_len


def _norm_mod_kernel(z_ref, g_ref, scl_ref, shl_ref, scc_ref, shc_ref, o_ref, *, tm, t_len, c_len):
    i = pl.program_id(0)
    mods = (scl_ref, shl_ref, scc_ref, shc_ref)

    def block(rows, first_row):
        inv = _inv_rms(z_ref, rows)
        is_ctx = _ctx_rows_of_block(i, tm, t_len, c_len, first_row)
        for c0 in range(0, z_ref.shape[1], NORM_COLS):
            h = _modulated_chunk(z_ref, g_ref, mods, rows, inv, is_ctx, c0)
            o_ref[rows, c0:c0 + NORM_COLS] = h.astype(o_ref.dtype)
    _for_row_blocks(tm, block)


HIGH_HALF = 0xFFFF0000


def _pack_halves(lo, hi):
    lo_bits = pltpu.bitcast(lo.astype(BF16).astype(F32), jnp.uint32)
    hi_bits = pltpu.bitcast(hi.astype(BF16).astype(F32), jnp.uint32)
    return (lo_bits >> jnp.uint32(16)) | (hi_bits & jnp.uint32(HIGH_HALF))


def _unpack_bf16_pairs(w):
    return pltpu.bitcast(w << jnp.uint32(16), F32), pltpu.bitcast(w & jnp.uint32(HIGH_HALF), F32)


def _norm_router_kernel(z_ref, g_ref, scl_ref, shl_ref, scc_ref, shc_ref, rw_ref, rb_ref,
                        hp_ref, idx_ref, rank_ref, prob_ref, cnt_ref, cnt_s, hi_s, lo_s, *, tm, t_len, c_len):
    i = pl.program_id(0)
    mods = (scl_ref, shl_ref, scc_ref, shc_ref)
    half = hp_ref.shape[1]

    @pl.when(i == 0)
    def _():
        cnt_s[...] = jnp.zeros(cnt_s.shape, F32)

    def block(rows, first_row):
        inv = _inv_rms(z_ref, rows)
        is_ctx = _ctx_rows_of_block(i, tm, t_len, c_len, first_row)
        for c0 in range(0, half, NORM_COLS):
            pair = []
            for off in (c0, half + c0):
                h = _modulated_chunk(z_ref, g_ref, mods, rows, inv, is_ctx, off)
                top = h.astype(BF16)
                hi_s[rows, off:off + NORM_COLS] = top
                lo_s[rows, off:off + NORM_COLS] = (h - top.astype(F32)).astype(BF16)
                pair.append(h)
            hp_ref[rows, c0:c0 + NORM_COLS] = _pack_halves(pair[0], pair[1])
    _for_row_blocks(tm, block)
    rw = rw_ref[...]
    w_hi = rw.astype(BF16)
    w_lo = (rw - w_hi.astype(F32)).astype(BF16)
    logits = (jnp.dot(hi_s[...], w_hi, preferred_element_type=F32)
              + jnp.dot(lo_s[...], w_hi, preferred_element_type=F32)
              + jnp.dot(hi_s[...], w_lo, preferred_element_type=F32) + rb_ref[...])
    n_exp = logits.shape[1]
    lane = lax.broadcasted_iota(jnp.int32, logits.shape, 1).astype(F32)
    work = logits
    picked = jnp.zeros(logits.shape, F32)
    firsts, vals = [], []
    for k in range(TOP_K):
        m = jnp.max(work, axis=-1, keepdims=True)
        first = jnp.min(jnp.where(work == m, lane, float(n_exp)), axis=-1, keepdims=True)
        hit = lane == first
        picked = jnp.where(hit, 1.0, picked)
        work = jnp.where(hit, -jnp.inf, work)
        firsts.append(first)
        vals.append(m)
    tri = (lax.broadcasted_iota(jnp.int32, (tm, tm), 0) > lax.broadcasted_iota(jnp.int32, (tm, tm), 1))
    before = jnp.dot(jnp.where(tri, 1.0, 0.0).astype(BF16), picked.astype(BF16), preferred_element_type=F32)
    before = before + cnt_s[...]
    ex = [jnp.exp(v - vals[0]) for v in vals]
    denom = ex[0] + ex[1] + ex[2] + ex[3]
    out_lane = lax.broadcasted_iota(jnp.int32, idx_ref.shape, 1)
    idx_out = jnp.zeros(idx_ref.shape, F32)
    rank_out = jnp.zeros(idx_ref.shape, F32)
    prob_out = jnp.zeros(idx_ref.shape, F32)
    for k in range(TOP_K):
        rank_k = jnp.sum(jnp.where(lane == firsts[k], before, 0.0), axis=-1, keepdims=True)
        idx_out = jnp.where(out_lane == k, firsts[k], idx_out)
        rank_out = jnp.where(out_lane == k, rank_k, rank_out)
        prob_out = jnp.where(out_lane == k, ex[k] / denom, prob_out)
    idx_ref[...] = idx_out.astype(jnp.int32)
    rank_ref[...] = rank_out.astype(jnp.int32)
    prob_ref[...] = prob_out
    cnt_s[...] = cnt_s[...] + jnp.sum(picked, axis=0, keepdims=True)
    cnt_ref[...] = cnt_s[...]


def _mod_specs(tm, t_len, n_batch, d, k_scale, k_shift):
    def lat(k):
        return pl.BlockSpec((None, None, 1, d), lambda i, *_: ((i * tm) // t_len, k, 0, 0))

    def ctx(k):
        return pl.BlockSpec((None, None, 1, d), lambda i, *_: (n_batch, k, 0, 0))
    return [lat(k_scale), lat(k_shift), ctx(k_scale), ctx(k_shift)]


def _norm_mod(z, g, mod, k_shift, k_scale, dims):
    n_batch, t_len, c_len = dims
    n, d = z.shape
    tm = _row_tile(t_len, 272)
    row_spec = pl.BlockSpec((tm, d), lambda i: (i, 0))
    return pl.pallas_call(
        functools.partial(_norm_mod_kernel, tm=tm, t_len=t_len, c_len=c_len),
        out_shape=jax.ShapeDtypeStruct((n, d), BF16),
        grid=(n // tm,),
        in_specs=[row_spec, pl.BlockSpec((1, d), lambda i: (0, 0))] + _mod_specs(tm, t_len, n_batch, d, k_scale, k_shift),
        out_specs=row_spec,
        compiler_params=_cparams("parallel"), name="norm_mod",
    )(z, g.reshape(1, d), mod, mod, mod, mod)


def _norm_router(z, g, mod, k_shift, k_scale, dims, rw, rb):
    n_batch, t_len, c_len = dims
    n, d = z.shape
    n_exp = rw.shape[1]
    tm = max(t for t in (LANES, 2 * LANES) if t_len % t == 0)
    row_spec = pl.BlockSpec((tm, d), lambda i: (i, 0))
    lane_spec = pl.BlockSpec((tm, LANES), lambda i: (i, 0))
    return pl.pallas_call(
        functools.partial(_norm_router_kernel, tm=tm, t_len=t_len, c_len=c_len),
        out_shape=(jax.ShapeDtypeStruct((n, d // 2), jnp.uint32), jax.ShapeDtypeStruct((n, LANES), jnp.int32),
                   jax.ShapeDtypeStruct((n, LANES), jnp.int32), jax.ShapeDtypeStruct((n, LANES), F32),
                   jax.ShapeDtypeStruct((1, n_exp), F32)),
        grid=(n // tm,),
        in_specs=[row_spec, pl.BlockSpec((1, d), lambda i: (0, 0))] + _mod_specs(tm, t_len, n_batch, d, k_scale, k_shift)
        + [pl.BlockSpec((d, n_exp), lambda i: (0, 0)), pl.BlockSpec((1, n_exp), lambda i: (0, 0))],
        out_specs=(pl.BlockSpec((tm, d // 2), lambda i: (i, 0)), lane_spec, lane_spec, lane_spec,
                   pl.BlockSpec((1, n_exp), lambda i: (0, 0))),
        scratch_shapes=[pltpu.VMEM((1, n_exp), F32), pltpu.VMEM((tm, d), BF16), pltpu.VMEM((tm, d), BF16)],
        compiler_params=_cparams("arbitrary"), name="norm_router",
    )(z, g.reshape(1, d), mod, mod, mod, mod, rw, rb.reshape(1, n_exp))


def _final_norm_kernel(z_ref, g_ref, o_ref):
    def block(rows, first_row):
        inv = _inv_rms(z_ref, rows)
        for c0 in range(0, z_ref.shape[1], NORM_COLS):
            cols = slice(c0, c0 + NORM_COLS)
            o_ref[rows, cols] = z_ref[rows, cols] * inv * g_ref[:, cols]
    _for_row_blocks(z_ref.shape[0], block)


def _final_norm(z, g, dims):
    n_batch, t_len, c_len = dims
    n, d = z.shape
    l_len = t_len - c_len
    tm = math.gcd(c_len, l_len)
    per_b = l_len // tm
    off = c_len // tm
    return pl.pallas_call(
        _final_norm_kernel,
        out_shape=jax.ShapeDtypeStruct((n_batch * l_len, d), F32),
        grid=(n_batch, per_b),
        in_specs=[pl.BlockSpec((tm, d), lambda b, s: (b * (t_len // tm) + off + s, 0)),
                  pl.BlockSpec((1, d), lambda b, s: (0, 0))],
        out_specs=pl.BlockSpec((tm, d), lambda b, s: (b * per_b + s, 0)),
        compiler_params=_cparams("parallel", "parallel"), name="final_norm",
    )(z, g.reshape(1, d)).reshape(n_batch, l_len, d)


def _mm_kernel(x_ref, w_ref, o_ref):
    o_ref[...] = jnp.dot(x_ref[...], w_ref[...], preferred_element_type=F32).astype(o_ref.dtype)


def _matmul(x, w, out_dtype, tm_target=1088, tn_target=512):
    n, k = x.shape
    n_out = w.shape[1]
    tm = _row_tile(n, tm_target)
    tn = _col_tile(n_out, tn_target)
    return pl.pallas_call(
        _mm_kernel,
        out_shape=jax.ShapeDtypeStruct((n, n_out), out_dtype),
        grid=(n // tm, n_out // tn),
        in_specs=[pl.BlockSpec((tm, k), lambda i, j: (i, 0)),
                  pl.BlockSpec((k, tn), lambda i, j: (0, j))],
        out_specs=pl.BlockSpec((tm, tn), lambda i, j: (i, j)),
        compiler_params=_cparams("parallel", "arbitrary"), name="matmul",
    )(x, w)


def _mm_residual_kernel(x_ref, w_ref, z_ref, gl_ref, gc_ref, o_ref, *, tm, t_len, c_len):
    acc = jnp.dot(x_ref[...], w_ref[...], preferred_element_type=F32)
    gate = jnp.where(_is_ctx_rows(pl.program_id(0), tm, t_len, c_len), gc_ref[...], gl_ref[...])
    o_ref[...] = z_ref[...] + gate * acc


def _matmul_residual(x, w, z, mod, k_gate, dims, tm_target=1088, tn_target=512):
    n_batch, t_len, c_len = dims
    n, k = x.shape
    d = w.shape[1]
    tm = _row_tile(t_len, tm_target)
    tn = _col_tile(d, tn_target)
    return pl.pallas_call(
        functools.partial(_mm_residual_kernel, tm=tm, t_len=t_len, c_len=c_len),
        out_shape=jax.ShapeDtypeStruct((n, d), F32),
        grid=(n // tm, d // tn),
        in_specs=[pl.BlockSpec((tm, k), lambda i, j: (i, 0)),
                  pl.BlockSpec((k, tn), lambda i, j: (0, j)),
                  pl.BlockSpec((tm, tn), lambda i, j: (i, j)),
                  pl.BlockSpec((None, None, 1, tn), lambda i, j: ((i * tm) // t_len, k_gate, 0, j)),
                  pl.BlockSpec((None, None, 1, tn), lambda i, j: (n_batch, k_gate, 0, j))],
        out_specs=pl.BlockSpec((tm, tn), lambda i, j: (i, j)),
        input_output_aliases={2: 0},
        compiler_params=_cparams("parallel", "arbitrary"), name="matmul_residual",
    )(x, w, z, mod, mod)


def _glu_kernel(x_ref, w_ref, b_ref, y_ref, o_ref):
    acc = jnp.dot(x_ref[...], w_ref[...], preferred_element_type=F32) + b_ref[...]
    o_ref[...] = (y_ref[...].astype(F32) * jax.nn.sigmoid(acc)).astype(o_ref.dtype)


def _glu(y, w, b):
    n, k = y.shape
    tm = _row_tile(n, 1088)
    tn = _col_tile(k, 512)
    return pl.pallas_call(
        _glu_kernel,
        out_shape=jax.ShapeDtypeStruct((n, k), BF16),
        grid=(n // tm, k // tn),
        in_specs=[pl.BlockSpec((tm, k), lambda i, j: (i, 0)),
                  pl.BlockSpec((k, tn), lambda i, j: (0, j)),
                  pl.BlockSpec((1, tn), lambda i, j: (0, j)),
                  pl.BlockSpec((tm, tn), lambda i, j: (i, j))],
        out_specs=pl.BlockSpec((tm, tn), lambda i, j: (i, j)),
        compiler_params=_cparams("parallel", "arbitrary"), name="s5_glu",
    )(y, w, b.reshape(1, k), y)


def _merge_kernel(h_ref, ya_ref, yb_ref, yc_ref, wg_ref, bg_ref, wb_ref, o_ref):
    h = h_ref[...]
    acc = None
    for b, y_ref in enumerate((ya_ref, yb_ref, yc_ref)):
        gate = jax.nn.sigmoid(jnp.dot(h, wg_ref[b], preferred_element_type=F32) + bg_ref[b])
        term = gate * jnp.dot(y_ref[...], wb_ref[b], preferred_element_type=F32)
        acc = term if acc is None else acc + term
    o_ref[...] = acc.astype(o_ref.dtype)


def _merge(h, ya, yb, yc, w_gate, b_gate, w_branch, tm_target=544, tn_target=256):
    n, d = h.shape
    w = ya.shape[1]
    tm = _row_tile(n, tm_target)
    tn = _col_tile(d, tn_target)
    y_spec = pl.BlockSpec((tm, w), lambda i, j: (i, 0))
    return pl.pallas_call(
        _merge_kernel,
        out_shape=jax.ShapeDtypeStruct((n, d), BF16),
        grid=(n // tm, d // tn),
        in_specs=[pl.BlockSpec((tm, d), lambda i, j: (i, 0)), y_spec, y_spec, y_spec,
                  pl.BlockSpec((N_BRANCH, d, tn), lambda i, j: (0, 0, j)),
                  pl.BlockSpec((N_BRANCH, 1, tn), lambda i, j: (0, 0, j)),
                  pl.BlockSpec((N_BRANCH, w, tn), lambda i, j: (0, 0, j))],
        out_specs=pl.BlockSpec((tm, tn), lambda i, j: (i, j)),
        compiler_params=_cparams("parallel", "arbitrary"), name="merge",
    )(h, ya, yb, yc, w_gate, b_gate.reshape(N_BRANCH, 1, d), w_branch)


def _static_chunks(start, size, step):
    return [(s, min(step, start + size - s)) for s in range(start, start + size, step)]


def _scan_block_real(a, b, row, reverse):
    for s in (1, 2, 4):
        if reverse:
            keep = row < SUBLANES - s
            shift = SUBLANES - s
        else:
            keep = row >= s
            shift = s
        a_prev = jnp.where(keep, pltpu.roll(a, shift, 0), 1.0)
        b_prev = jnp.where(keep, pltpu.roll(b, shift, 0), 0.0)
        b = a * b_prev + b
        a = a * a_prev
    return a, b


LRU_BLOCKS_PER_STEP = 2


def _lru_kernel(x_ref, g_ref, cw_ref, cb_ref, wa_ref, ba_ref, wx_ref, bx_ref, lam_ref, o_ref,
                xp_s, xc_s, yf_s, yb_s, sa0, sb0, sa1, sb1, ta0, tb0, ta1, tb1, *, c_len, l_len, rb):
    t_len = c_len + l_len
    width = x_ref.shape[1]
    bw = wa_ref.shape[-1]
    cw = cw_ref[...]
    cb = cb_ref[...]
    zeros8 = jnp.zeros((SUBLANES, width), F32)
    for seg0, seg_n in ((0, c_len), (c_len, l_len)):
        xp_s[0:SUBLANES, :] = zeros8
        xp_s[SUBLANES + seg_n:2 * SUBLANES + seg_n, :] = zeros8
        for s0, sn in _static_chunks(0, seg_n, rb):
            xp_s[SUBLANES + s0:SUBLANES + s0 + sn, :] = x_ref[seg0 + s0:seg0 + s0 + sn, :].astype(F32)
        for s0, sn in _static_chunks(0, seg_n, rb):
            acc = cb + xp_s[SUBLANES - 1 + s0:SUBLANES - 1 + s0 + sn, :] * cw[0:1]
            for j in range(1, CONV_W):
                acc = acc + xp_s[SUBLANES - 1 + j + s0:SUBLANES - 1 + j + s0 + sn, :] * cw[j:j + 1]
            xc_s[seg0 + s0:seg0 + s0 + sn, :] = acc

    row = lax.broadcasted_iota(jnp.int32, (SUBLANES, width), 0)
    n_ch = t_len // rb
    c_ch = c_len // rb
    n_sub = rb // SUBLANES
    stages = ((sa0, sb0, sa1, sb1), (ta0, tb0, ta1, tb1))
    sp = [_softplus(-lam_ref[d:d + 1, :]) for d in range(2)]

    def chunk_rows(c):
        return pl.ds(pl.multiple_of(c * rb, rb), rb)

    def chunks_of(v):
        return v, jnp.where(v < c_ch, c_ch - 1 - v, n_ch + c_ch - 1 - v)

    def gates(v, stage):
        real = v < n_ch
        for d, c in enumerate(chunks_of(v)):
            rows = chunk_rows(jnp.where(real, c, 0))
            for k in range(width // bw):
                cols = slice(k * bw, (k + 1) * bw)
                xc = xc_s[rows, cols]
                xb = xc.astype(BF16)
                rr = jax.nn.sigmoid(jnp.dot(xb, wa_ref[d, k], preferred_element_type=F32) + ba_ref[d:d + 1, cols])
                ii = jax.nn.sigmoid(jnp.dot(xb, wx_ref[d, k], preferred_element_type=F32) + bx_ref[d:d + 1, cols])
                a = jnp.exp((-LRU_C) * rr * sp[d][:, cols])
                stage[2 * d][:, cols] = a
                stage[2 * d + 1][:, cols] = jnp.sqrt(1.0 - a * a) * (ii * xc)

    def scan(v, stage, carry):
        h_f, h_b = carry
        real = v < n_ch
        c_f, c_b = chunks_of(v)
        base_f = jnp.where(real, c_f, n_ch) * rb
        base_b = jnp.where(real, c_b, n_ch) * rb
        for blk in range(n_sub):
            lo_f = blk * SUBLANES
            lo_b = (n_sub - 1 - blk) * SUBLANES
            af, bf = _scan_block_real(stage[0][lo_f:lo_f + SUBLANES, :], stage[1][lo_f:lo_f + SUBLANES, :], row, False)
            ab, bb = _scan_block_real(stage[2][lo_b:lo_b + SUBLANES, :], stage[3][lo_b:lo_b + SUBLANES, :], row, True)
            hs_f = af * h_f + bf
            hs_b = ab * h_b + bb
            yf_s[pl.ds(pl.multiple_of(base_f + lo_f, SUBLANES), SUBLANES), :] = hs_f
            yb_s[pl.ds(pl.multiple_of(base_b + lo_b, SUBLANES), SUBLANES), :] = hs_b
            h_f, h_b = hs_f[SUBLANES - 1:SUBLANES, :], hs_b[0:1, :]
        return h_f, h_b

    gates(0, stages[0])

    def step(k, carry):
        for ph in range(2):
            v = 2 * k + ph
            gates(v + 1, stages[(ph + 1) % 2])
            carry = scan(v, stages[ph], carry)
        return carry
    h0 = jnp.zeros((1, width), F32)
    lax.fori_loop(0, (n_ch + 1) // 2, step, (h0, h0))

    def finish(r, carry):
        rows = chunk_rows(r)
        y = yf_s[rows, :] + yb_s[rows, :]
        o_ref[rows, :] = (jax.nn.gelu(g_ref[rows, :].astype(F32)) * y).astype(o_ref.dtype)
        return carry
    lax.fori_loop(0, n_ch, finish, 0)


def _lru_branch(u, conv_w, conv_b, w_a, b_a, w_x, b_x, lam, dims):
    n_batch, t_len, c_len = dims
    l_len = t_len - c_len
    n = u.shape[0]
    width = conv_w.shape[1]
    bw = width // LRU_BLOCKS
    per = LRU_BLOCKS_PER_STEP
    cw = per * bw
    rb = math.gcd(math.gcd(c_len, l_len), 256)
    vec2 = pl.BlockSpec((2, cw), lambda b, k: (0, k))
    wspec = pl.BlockSpec((2, per, bw, bw), lambda b, k: (0, k, 0, 0))
    return pl.pallas_call(
        functools.partial(_lru_kernel, c_len=c_len, l_len=l_len, rb=rb),
        out_shape=jax.ShapeDtypeStruct((n, width), BF16),
        grid=(n_batch, LRU_BLOCKS // per),
        in_specs=[pl.BlockSpec((t_len, cw), lambda b, k: (b, k)),
                  pl.BlockSpec((t_len, cw), lambda b, k: (b, LRU_BLOCKS // per + k)),
                  pl.BlockSpec((CONV_W, cw), lambda b, k: (0, k)),
                  pl.BlockSpec((1, cw), lambda b, k: (0, k)),
                  wspec, vec2, wspec, vec2, vec2],
        out_specs=pl.BlockSpec((t_len, cw), lambda b, k: (b, k)),
        scratch_shapes=[pltpu.VMEM((l_len + 2 * SUBLANES, cw), F32), pltpu.VMEM((t_len, cw), F32),
                        pltpu.VMEM((t_len + rb, cw), F32), pltpu.VMEM((t_len + rb, cw), F32)]
        + [pltpu.VMEM((rb, cw), F32)] * 8,
        compiler_params=_cparams("parallel", "parallel"), name="rglru",
    )(u, u, conv_w, conv_b.reshape(1, width), w_a.astype(BF16), b_a, w_x.astype(BF16), b_x, lam)


RET_HEADS_PER_STEP = 2


def _ret_kernel(q_ref, k_ref, v_ref, g_ref, cos_ref, sin_ref, dec_ref, o_ref,
                qs_s, ks_s, sb_s, dm_s, vec_s, *, c_len, l_len, rb):
    t_len = c_len + l_len
    hd = RET_HEAD_DIM
    heads = q_ref.shape[1] // hd
    ch = RET_CHUNK
    k_scale = hd ** -0.5
    ri = lax.broadcasted_iota(jnp.int32, (ch, ch), 0).astype(F32)
    ci = lax.broadcasted_iota(jnp.int32, (ch, ch), 1).astype(F32)
    diff = ri - ci
    pos = lax.broadcasted_iota(jnp.int32, (ch, hd), 0).astype(F32)
    cd_f, cd_b = [], []
    for h in range(heads):
        dec = dec_ref[:, h]
        lg_f = -_softplus(-dec[0])
        lg_b = -_softplus(-dec[1])
        dm_s[h] = (jnp.where(diff >= 0, jnp.exp(lg_f * jnp.maximum(diff, 0.0)), 0.0)
                   + jnp.where(diff <= 0, jnp.exp(lg_b * jnp.maximum(-diff, 0.0)), 0.0))
        vec_s[h, 0] = jnp.exp(lg_f * (ch - 1.0 - pos))
        vec_s[h, 1] = jnp.exp(lg_f * (pos + 1.0))
        vec_s[h, 2] = jnp.exp(lg_b * pos)
        vec_s[h, 3] = jnp.exp(lg_b * (ch - pos))
        cd_f.append(jnp.exp(lg_f * float(ch)))
        cd_b.append(jnp.exp(lg_b * float(ch)))

    def cols(h):
        return slice(h * hd, (h + 1) * hd)

    for s0, sn in _static_chunks(0, c_len, rb):
        qs_s[s0:s0 + sn, :] = q_ref[s0:s0 + sn, :].astype(F32)
        ks_s[s0:s0 + sn, :] = k_ref[s0:s0 + sn, :].astype(F32) * k_scale
    lane = lax.broadcasted_iota(jnp.int32, (rb, hd), 1)
    low = (lane % (hd // 2)) < (hd // 4)

    def rope(r, carry):
        src = pl.ds(pl.multiple_of(c_len + r * rb, rb), rb)
        tab = pl.ds(pl.multiple_of(r * rb, rb), rb)
        cs = cos_ref[tab, :]
        sn = sin_ref[tab, :]
        for ref, dst, scale in ((q_ref, qs_s, 1.0), (k_ref, ks_s, k_scale)):
            for h in range(heads):
                x = ref[src, cols(h)].astype(F32)
                partner = jnp.where(low, pltpu.roll(x, hd - hd // 4, 1), pltpu.roll(x, hd // 4, 1))
                y = x * cs + partner * sn
                dst[src, cols(h)] = y * scale if scale != 1.0 else y
        return carry
    lax.fori_loop(0, l_len // rb, rope, 0)

    def chunk_rows(c):
        return pl.ds(pl.multiple_of(c * ch, ch), ch)

    def kv_state(rows, h, kdec):
        kd = (ks_s[rows, cols(h)] * kdec).T.astype(BF16)
        return jnp.dot(kd, v_ref[rows, cols(h)], preferred_element_type=F32)

    n_ch = t_len // ch
    c_ch = c_len // ch

    def back(i, states, lo, hi):
        c = hi - 1 - i
        out = []
        for h in range(heads):
            sb_s[h, c] = states[h]
            out.append(cd_b[h] * states[h] + kv_state(chunk_rows(c), h, vec_s[h, 2]))
        return tuple(out)
    zero = tuple(jnp.zeros((hd, hd), F32) for _ in range(heads))
    s_ctx = lax.fori_loop(0, c_ch, functools.partial(back, lo=0, hi=c_ch), zero)
    lax.fori_loop(0, n_ch - c_ch, functools.partial(back, lo=c_ch, hi=n_ch), s_ctx)

    def fwd(c, states):
        rows = chunk_rows(c)
        out = []
        for h in range(heads):
            s = states[h]
            q = qs_s[rows, cols(h)]
            qb = q.astype(BF16)
            kb = ks_s[rows, cols(h)].astype(BF16)
            scores = lax.dot_general(qb, kb, (((1,), (1,)), ((), ())), preferred_element_type=F32) * dm_s[h]
            o = jnp.dot(scores.astype(BF16), v_ref[rows, cols(h)], preferred_element_type=F32)
            o = o + jnp.dot((q * vec_s[h, 1]).astype(BF16), s.astype(BF16), preferred_element_type=F32)
            o = o + jnp.dot((q * vec_s[h, 3]).astype(BF16), sb_s[h, c].astype(BF16), preferred_element_type=F32)
            o = o * lax.rsqrt(jnp.mean(o * o, axis=-1, keepdims=True) + NORM_EPS)
            g = g_ref[rows, cols(h)].astype(F32)
            o_ref[rows, cols(h)] = (o * (g * jax.nn.sigmoid(g))).astype(o_ref.dtype)
            out.append(cd_f[h] * s + kv_state(rows, h, vec_s[h, 0]))
        return tuple(out)
    lax.fori_loop(0, n_ch, fwd, zero)


def _rope_tables(l_len, hd):
    t = jnp.arange(l_len, dtype=jnp.int32)
    row = (t // GRID_W).astype(F32)
    col = (t % GRID_W).astype(F32)
    quarter = hd // 4
    freqs = ROPE_BASE ** (-jnp.arange(quarter, dtype=F32) / quarter)
    ang_r = row[:, None] * freqs
    ang_c = col[:, None] * freqs
    cos = jnp.concatenate([jnp.cos(ang_r)] * 2 + [jnp.cos(ang_c)] * 2, axis=-1)
    sin = jnp.concatenate([-jnp.sin(ang_r), jnp.sin(ang_r), -jnp.sin(ang_c), jnp.sin(ang_c)], axis=-1)
    return cos, sin


def _ret_branch(u, cos, sin, decay, dims):
    n_batch, t_len, c_len = dims
    l_len = t_len - c_len
    n = u.shape[0]
    hd = RET_HEAD_DIM
    n_heads = decay.shape[1]
    rb = math.gcd(math.gcd(c_len, l_len), 256)

    per = RET_HEADS_PER_STEP
    steps = n_heads // per
    wide = per * hd

    def col(split):
        return pl.BlockSpec((t_len, wide), lambda b, h: (b, split * steps + h))
    tab = pl.BlockSpec((l_len, hd), lambda b, h: (0, 0))
    return pl.pallas_call(
        functools.partial(_ret_kernel, c_len=c_len, l_len=l_len, rb=rb),
        out_shape=jax.ShapeDtypeStruct((n, n_heads * hd), BF16),
        grid=(n_batch, steps),
        in_specs=[col(2), col(3), col(4), col(5), tab, tab,
                  pl.BlockSpec((2, per, 1, 1), lambda b, h: (0, h, 0, 0))],
        out_specs=pl.BlockSpec((t_len, wide), lambda b, h: (b, h)),
        scratch_shapes=[pltpu.VMEM((t_len, wide), F32), pltpu.VMEM((t_len, wide), F32),
                        pltpu.VMEM((per, t_len // RET_CHUNK, hd, hd), F32),
                        pltpu.VMEM((per, RET_CHUNK, RET_CHUNK), F32),
                        pltpu.VMEM((per, 4, RET_CHUNK, hd), F32)],
        compiler_params=_cparams("parallel", "parallel"), name="retention",
    )(u, u, u, u, cos, sin, decay.reshape(2, n_heads, 1, 1))


def _s5_disc_kernel(are_ref, aim_ref, ldt_ref, bre_ref, bim_ref, pre_ref, pim_ref, bbre_ref, bbim_ref):
    lam_re = jnp.minimum(are_ref[...], -1e-4)
    lam_im = aim_ref[...]
    dt = jnp.exp(ldt_ref[...])
    z_re = lam_re * dt
    z_im = lam_im * dt
    mag = jnp.exp(z_re)
    ab_re = mag * jnp.cos(z_im)
    ab_im = mag * jnp.sin(z_im)
    den = lam_re * lam_re + lam_im * lam_im
    n_re = ab_re - 1.0
    co_re = (n_re * lam_re + ab_im * lam_im) / den
    co_im = (ab_im * lam_re - n_re * lam_im) / den
    for i in range(bre_ref.shape[0]):
        b_re = bre_ref[i]
        b_im = bim_ref[i]
        bbre_ref[i] = co_re * b_re - co_im * b_im
        bbim_ref[i] = co_re * b_im + co_im * b_re
    p_re, p_im = ab_re, ab_im
    pre_ref[0] = p_re
    pim_ref[0] = p_im
    for j in range(1, SUBLANES):
        p_re, p_im = p_re * ab_re - p_im * ab_im, p_re * ab_im + p_im * ab_re
        pre_ref[j] = p_re
        pim_ref[j] = p_im


def _s5_discretise(a_re, a_im, log_dt, b_re, b_im):
    two, g, p = a_re.shape
    i = b_re.shape[-1]
    rows = two * g
    full2 = pl.BlockSpec((rows, p), lambda: (0, 0))
    full3 = pl.BlockSpec((i, rows, p), lambda: (0, 0, 0))
    pw = pl.BlockSpec((SUBLANES, rows, p), lambda: (0, 0, 0))

    def input_major(m):
        return jnp.transpose(m, (3, 0, 1, 2)).reshape(i, rows, p)
    pw_re, pw_im, bb_re, bb_im = pl.pallas_call(
        _s5_disc_kernel,
        out_shape=(jax.ShapeDtypeStruct((SUBLANES, rows, p), F32), jax.ShapeDtypeStruct((SUBLANES, rows, p), F32),
                   jax.ShapeDtypeStruct((i, rows, p), F32), jax.ShapeDtypeStruct((i, rows, p), F32)),
        in_specs=[full2, full2, pl.BlockSpec((rows, 1), lambda: (0, 0)), full3, full3],
        out_specs=(pw, pw, full3, full3),
        name="s5_discretise",
    )(a_re.reshape(rows, p), a_im.reshape(rows, p), log_dt.reshape(rows, 1), input_major(b_re), input_major(b_im))
    return pw_re, pw_im, jnp.swapaxes(bb_re, 0, 1), jnp.swapaxes(bb_im, 0, 1)


def _s5_kernel(u_ref, bb_ref, cb_ref, pw_ref, dsk_ref, o_ref, buf0, buf1, buf2, y_s, lv_s, *, c_len, l_len, rb):
    t_len = c_len + l_len
    ns = buf0.shape[1] // 2
    bufs = (buf0, buf1, buf2)
    row = lax.broadcasted_iota(jnp.int32, (SUBLANES, ns), 0)
    dsk = dsk_ref[...]

    def skip(r, carry):
        rows = pl.ds(pl.multiple_of(r * rb, rb), rb)
        y_s[rows, :] = dsk * u_ref[rows, :].astype(F32)
        return carry
    lax.fori_loop(0, t_len // rb, skip, 0)

    y_s[pl.ds(t_len, rb), :] = jnp.zeros((rb, y_s.shape[1]), F32)
    buf2[...] = jnp.zeros(buf2.shape, F32)
    n_ch = t_len // rb
    c_ch = c_len // rb
    n_iter = (n_ch + 3) // 3

    def chunk_rows(c):
        return pl.ds(c * rb, rb) if isinstance(c, int) else pl.ds(pl.multiple_of(c * rb, rb), rb)

    for d in range(2):
        reverse = d == 1
        bb = bb_ref[d]
        cb = cb_ref[d]

        for lvl, s in enumerate((1, 2, 4)):
            keep = (row < SUBLANES - s) if reverse else (row >= s)
            for part in range(2):
                lv_s[lvl, part] = jnp.where(keep, pw_ref[d, part, s - 1:s, :], 0.0)
        for part in range(2):
            if reverse:
                for j in range(SUBLANES):
                    lv_s[3, part, j:j + 1, :] = pw_ref[d, part, SUBLANES - 1 - j:SUBLANES - j, :]
            else:
                lv_s[3, part] = pw_ref[d, part]

        def chunk_of(i, reverse=reverse):
            if not reverse:
                return i
            return jnp.where(i < c_ch, c_ch - 1 - i, n_ch + c_ch - 1 - i)

        def drive(i, buf, bb=bb):
            src = jnp.where(i < n_ch, chunk_of(i), 0)
            buf[...] = jnp.dot(u_ref[chunk_rows(src), :], bb, preferred_element_type=F32)

        def readout(i, buf, cb=cb):
            dst = jnp.where(jnp.logical_and(i >= 0, i < n_ch), chunk_of(i), n_ch)
            rows = chunk_rows(dst)
            y = jnp.dot(buf[:, :ns].astype(BF16), cb[:ns, :], preferred_element_type=F32)
            y = y + jnp.dot(buf[:, ns:].astype(BF16), cb[ns:, :], preferred_element_type=F32)
            y_s[rows, :] = y_s[rows, :] + y

        def scan_chunk(buf, carry, reverse=reverse):
            cr, ci = carry
            order = range(rb // SUBLANES - 1, -1, -1) if reverse else range(rb // SUBLANES)
            for blk in order:
                rows = slice(blk * SUBLANES, (blk + 1) * SUBLANES)
                hr = buf[rows, :ns]
                hi_ = buf[rows, ns:]
                for lvl, s in enumerate((1, 2, 4)):
                    shift = SUBLANES - s if reverse else s
                    ar = lv_s[lvl, 0]
                    ai = lv_s[lvl, 1]
                    sr = pltpu.roll(hr, shift, 0)
                    si = pltpu.roll(hi_, shift, 0)
                    hr, hi_ = hr + (ar * sr - ai * si), hi_ + (ar * si + ai * sr)
                pr = lv_s[3, 0]
                pi = lv_s[3, 1]
                hr, hi_ = hr + (pr * cr - pi * ci), hi_ + (pr * ci + pi * cr)
                buf[rows, :ns] = hr
                buf[rows, ns:] = hi_
                if reverse:
                    cr, ci = hr[0:1, :], hi_[0:1, :]
                else:
                    cr, ci = hr[SUBLANES - 1:SUBLANES, :], hi_[SUBLANES - 1:SUBLANES, :]
            return cr, ci

        drive(0, bufs[0])

        def step(k, carry):
            for ph in range(3):
                i = 3 * k + ph
                drive(i + 1, bufs[(ph + 1) % 3])
                carry = scan_chunk(bufs[ph], carry)
                readout(i - 1, bufs[(ph + 2) % 3])
            return carry
        zero = (jnp.zeros((1, ns), F32), jnp.zeros((1, ns), F32))
        lax.fori_loop(0, n_iter, step, zero)

    def finish(r, carry):
        rows = pl.ds(pl.multiple_of(r * rb, rb), rb)
        o_ref[rows, :] = jax.nn.gelu(y_s[rows, :]).astype(o_ref.dtype)
        return carry
    lax.fori_loop(0, t_len // rb, finish, 0)


def _s5_branch(u, a_re, a_im, log_dt, b_re, b_im, c_re, c_im, d_skip, dims):
    n_batch, t_len, c_len = dims
    l_len = t_len - c_len
    n = u.shape[0]
    _, g, p = a_re.shape
    i = b_re.shape[-1]
    width = g * i
    gpb = LANES // i
    nb = g // gpb
    ns = gpb * p
    pw_re, pw_im, bb_re, bb_im = _s5_discretise(a_re, a_im, log_dt, b_re, b_im)
    eye = jnp.eye(gpb, dtype=F32)

    def blockdiag_in(m):
        m = m.reshape(2, nb, gpb, i, p)
        return jnp.einsum('dbgip,gh->dbgihp', m, eye).reshape(2, nb, gpb * i, ns)
    bb = jnp.concatenate([blockdiag_in(bb_re), blockdiag_in(bb_im)], axis=-1).astype(BF16)

    def blockdiag_out(m):
        m = m.reshape(2, nb, gpb, i, p)
        return jnp.einsum('dbgip,gh->dbgphi', m, eye).reshape(2, nb, ns, gpb * i)
    cb = jnp.concatenate([blockdiag_out(c_re), -blockdiag_out(c_im)], axis=-2).astype(BF16)
    pw = jnp.stack([pw_re.reshape(SUBLANES, 2, g * p), pw_im.reshape(SUBLANES, 2, g * p)], axis=0)
    pw = jnp.transpose(pw, (2, 0, 1, 3))
    rb = math.gcd(math.gcd(c_len, l_len), 256)
    cols = (N_IN_SPLITS - 1) * width // LANES
    return pl.pallas_call(
        functools.partial(_s5_kernel, c_len=c_len, l_len=l_len, rb=rb),
        out_shape=jax.ShapeDtypeStruct((n, width), BF16),
        grid=(n_batch, nb),
        in_specs=[pl.BlockSpec((t_len, LANES), lambda b, k: (b, cols + k)),
                  pl.BlockSpec((2, None, LANES, 2 * ns), lambda b, k: (0, k, 0, 0)),
                  pl.BlockSpec((2, None, 2 * ns, LANES), lambda b, k: (0, k, 0, 0)),
                  pl.BlockSpec((2, 2, SUBLANES, ns), lambda b, k: (0, 0, 0, k)),
                  pl.BlockSpec((1, LANES), lambda b, k: (0, k))],
        out_specs=pl.BlockSpec((t_len, LANES), lambda b, k: (b, k)),
        scratch_shapes=[pltpu.VMEM((rb, 2 * ns), F32), pltpu.VMEM((rb, 2 * ns), F32), pltpu.VMEM((rb, 2 * ns), F32),
                        pltpu.VMEM((t_len + rb, LANES), F32),
                        pltpu.VMEM((4, 2, SUBLANES, ns), F32)],
        compiler_params=_cparams("parallel", "parallel"), name="s5",
    )(u, bb, cb, pw, d_skip.reshape(1, width))


EXPERT_TILE = 256
EXPERT_LOOKAHEAD = 2
GATHER_UNROLL = 8


def _expert_kernel(te_ref, nu_ref, tok_ref, hp_hbm, w1_ref, b1_ref, w2_ref, b2_ref, y_ref, xbuf, xb, act_s, sem,
                   *, tme, ff, dc):
    j = pl.program_id(0)
    n_used = nu_ref[0]
    n_buf = xbuf.shape[0]
    ahead = n_buf - 1
    slot = j % n_buf
    half = xbuf.shape[2]

    def row_copy(tile, r):
        tok = tok_ref[tile * tme + r]
        s = tile % n_buf
        return pltpu.make_async_copy(hp_hbm.at[pl.ds(tok, 1), :], xbuf.at[s, pl.ds(r, 1), :], sem.at[s])

    def start_tile(tile):
        def body(r, c):
            row_copy(tile, r).start()
            return c
        lax.fori_loop(0, tme, body, 0, unroll=GATHER_UNROLL)

    @pl.when(j == 0)
    def _():
        for t in range(ahead):
            start_tile(t)

    def wait_tile(tile):
        def body(r, c):
            row_copy(tile, r).wait()
            return c
        lax.fori_loop(0, tme, body, 0, unroll=GATHER_UNROLL)

    @pl.when(j < n_used)
    def _():
        wait_tile(j)
        rc = 2 * SUBLANES

        def unpack(q, c):
            rows = pl.ds(pl.multiple_of(q * rc, rc), rc)
            for c0 in range(0, half, COMBINE_COLS):
                lo, hi = _unpack_bf16_pairs(xbuf[slot, rows, c0:c0 + COMBINE_COLS])
                xb[rows, c0:c0 + COMBINE_COLS] = lo.astype(BF16)
                xb[rows, half + c0:half + c0 + COMBINE_COLS] = hi.astype(BF16)
            return c
        lax.fori_loop(0, tme // rc, unpack, 0)
        for r in range(tme):
            row_copy(j + ahead, r).start()
        hu = jnp.dot(xb[...], w1_ref[...], preferred_element_type=F32) + b1_ref[...]
        gate = jnp.minimum(hu[:, :ff], SWIGLU_LIMIT)
        up = jnp.clip(hu[:, ff:], -SWIGLU_LIMIT, SWIGLU_LIMIT)
        act_s[...] = (gate * jax.nn.sigmoid(SWIGLU_ALPHA * gate) * (up + 1.0)).astype(BF16)
        for c0 in range(0, half, dc):
            lo = jnp.dot(act_s[...], w2_ref[:, c0:c0 + dc], preferred_element_type=F32) + b2_ref[:, c0:c0 + dc]
            hi = (jnp.dot(act_s[...], w2_ref[:, half + c0:half + c0 + dc], preferred_element_type=F32)
                  + b2_ref[:, half + c0:half + c0 + dc])
            y_ref[:, c0:c0 + dc] = _pack_halves(lo, hi)

    @pl.when(jnp.logical_and(j >= n_used, j < n_used + ahead))
    def _():
        wait_tile(j)

    @pl.when(j >= n_used)
    def _():
        y_ref[...] = jnp.zeros(y_ref.shape, y_ref.dtype)


def _experts(hp, tile_expert, n_used, tok_of_slot, w1, b1, w2, b2):
    n_exp, d, ff2 = w1.shape
    half = d // 2
    n_tiles = tile_expert.shape[0]
    tme = EXPERT_TILE
    grid_spec = pltpu.PrefetchScalarGridSpec(
        num_scalar_prefetch=3,
        grid=(n_tiles,),
        in_specs=[pl.BlockSpec(memory_space=pl.ANY),
                  pl.BlockSpec((None, d, ff2), lambda j, te, nu, tok: (te[j], 0, 0)),
                  pl.BlockSpec((None, 1, ff2), lambda j, te, nu, tok: (te[j], 0, 0)),
                  pl.BlockSpec((None, ff2 // 2, d), lambda j, te, nu, tok: (te[j], 0, 0)),
                  pl.BlockSpec((None, 1, d), lambda j, te, nu, tok: (te[j], 0, 0))],
        out_specs=pl.BlockSpec((tme, half), lambda j, te, nu, tok: (j, 0)),
        scratch_shapes=[pltpu.VMEM((EXPERT_LOOKAHEAD + 1, tme, half), jnp.uint32), pltpu.VMEM((tme, d), BF16),
                        pltpu.VMEM((tme, ff2 // 2), BF16),
                        pltpu.SemaphoreType.DMA((EXPERT_LOOKAHEAD + 1,))],
    )
    return pl.pallas_call(
        functools.partial(_expert_kernel, tme=tme, ff=ff2 // 2, dc=_col_tile(half, 512)),
        out_shape=jax.ShapeDtypeStruct((n_tiles * tme, half), jnp.uint32),
        grid_spec=grid_spec,
        compiler_params=_cparams("arbitrary"), name="experts",
    )(tile_expert, n_used, tok_of_slot, hp, w1, b1.reshape(n_exp, 1, ff2), w2, b2.reshape(n_exp, 1, d))


COMBINE_ROWS = SUBLANES
COMBINE_COLS = 4 * LANES


def _combine_kernel(pos_ref, y_hbm, prob_ref, z_ref, gl_ref, gc_ref, o_ref, ybuf, sem, *, tm, t_len, c_len):
    i = pl.program_id(0)
    n_steps = pl.num_programs(0)
    slot = i % 2
    half = ybuf.shape[3]

    def row_copy(tile, r, k, s):
        row = pos_ref[(tile * tm + r) * TOP_K + k]
        return pltpu.make_async_copy(y_hbm.at[pl.ds(row, 1), :], ybuf.at[s, k, pl.ds(r, 1), :], sem.at[s])

    def start_tile(tile, s):
        def body(r, c):
            for k in range(TOP_K):
                row_copy(tile, r, k, s).start()
            return c
        lax.fori_loop(0, tm, body, 0, unroll=GATHER_UNROLL // TOP_K)

    @pl.when(i == 0)
    def _():
        start_tile(i, slot)

    def wait_tile(tile, s):
        def body(r, c):
            for k in range(TOP_K):
                row_copy(tile, r, k, s).wait()
            return c
        lax.fori_loop(0, tm, body, 0, unroll=GATHER_UNROLL // TOP_K)
    wait_tile(i, slot)

    rc = COMBINE_ROWS
    cw = COMBINE_COLS
    nxt = jnp.minimum(i + 1, n_steps - 1)

    def reduce_rows(q, c):
        for r in range(rc):
            for k in range(TOP_K):
                row_copy(nxt, q * rc + r, k, 1 - slot).start()
        rows = pl.ds(q * rc, rc)
        prob = prob_ref[rows, :]
        row = (i * tm) % t_len + q * rc + lax.broadcasted_iota(jnp.int32, (rc, 1), 0)
        is_ctx = row < c_len
        for c0 in range(0, half, cw):
            acc_lo = acc_hi = None
            for k in range(TOP_K):
                lo, hi = _unpack_bf16_pairs(ybuf[slot, k, rows, c0:c0 + cw])
                p = prob[:, k:k + 1]
                acc_lo = p * lo if acc_lo is None else acc_lo + p * lo
                acc_hi = p * hi if acc_hi is None else acc_hi + p * hi
            for acc, off in ((acc_lo, c0), (acc_hi, half + c0)):
                gate = jnp.where(is_ctx, gc_ref[:, off:off + cw], gl_ref[:, off:off + cw])
                o_ref[rows, off:off + cw] = z_ref[rows, off:off + cw] + gate * acc
        return c
    for q in range(tm // rc):
        reduce_rows(q, 0)

    @pl.when(i == n_steps - 1)
    def _():
        wait_tile(nxt, 1 - slot)


def _combine(y_sorted, pos, prob, z, mod, k_gate, dims):
    n_batch, t_len, c_len = dims
    n, d = z.shape
    half = d // 2
    tm = max(t for t in (LANES, 2 * LANES) if t_len % t == 0)
    grid_spec = pltpu.PrefetchScalarGridSpec(
        num_scalar_prefetch=1,
        grid=(n // tm,),
        in_specs=[pl.BlockSpec(memory_space=pl.ANY),
                  pl.BlockSpec((tm, LANES), lambda i, pos: (i, 0)),
                  pl.BlockSpec((tm, d), lambda i, pos: (i, 0)),
                  pl.BlockSpec((None, None, 1, d), lambda i, pos: ((i * tm) // t_len, k_gate, 0, 0)),
                  pl.BlockSpec((None, None, 1, d), lambda i, pos: (n_batch, k_gate, 0, 0))],
        out_specs=pl.BlockSpec((tm, d), lambda i, pos: (i, 0)),
        scratch_shapes=[pltpu.VMEM((2, TOP_K, tm, half), jnp.uint32), pltpu.SemaphoreType.DMA((2,))],
    )
    return pl.pallas_call(
        functools.partial(_combine_kernel, tm=tm, t_len=t_len, c_len=c_len),
        out_shape=jax.ShapeDtypeStruct((n, d), F32),
        grid_spec=grid_spec,
        input_output_aliases={3: 0},
        compiler_params=_cparams("arbitrary"), name="moe_combine",
    )(pos, y_sorted, prob, z, mod, mod)


def _route(idx, rank, counts, n_exp):
    n = idx.shape[0]
    tme = EXPERT_TILE
    n_tiles = (n * TOP_K + n_exp * (tme - 1)) // tme + EXPERT_LOOKAHEAD
    counts = counts.reshape(n_exp).astype(jnp.int32)
    padded = ((counts + tme - 1) // tme) * tme
    ends = jnp.cumsum(padded)
    base = ends - padded
    idx = idx[:, :TOP_K]
    pos = jnp.take(base, idx) + rank[:, :TOP_K]
    tile_start = jnp.arange(n_tiles, dtype=jnp.int32) * tme
    tile_expert = jnp.sum((ends[None, :] <= tile_start[:, None]).astype(jnp.int32), axis=1)
    tile_expert = jnp.minimum(tile_expert, n_exp - 1)
    n_used = (ends[-1:] // tme).astype(jnp.int32)
    token = jnp.broadcast_to(jnp.arange(n, dtype=jnp.int32)[:, None], (n, TOP_K))
    tok_of_slot = jnp.zeros((n_tiles * tme,), jnp.int32).at[pos.reshape(-1)].set(
        token.reshape(-1), unique_indices=True, indices_are_sorted=False)
    return pos.reshape(-1).astype(jnp.int32), tile_expert, n_used, tok_of_slot


def _moe_sublayer(z, g, mod, dims, rw, rb, w1, b1, w2, b2):
    hp, idx, rank, prob, counts = _norm_router(z, g, mod, 3, 4, dims, rw, rb)
    pos, tile_expert, n_used, tok_of_slot = _route(idx, rank, counts, rw.shape[1])
    y_sorted = _experts(hp, tile_expert, n_used, tok_of_slot, w1, b1, w2, b2)
    return _combine(y_sorted, pos, prob, z, mod, 5, dims)


def _layer(z, p, cos, sin, dims):
    d = z.shape[1]
    mod = p['mod'].reshape(p['mod'].shape[0], N_MOD, 1, d)
    h = _norm_mod(z, p['norm_mix_g'], mod, 0, 1, dims)
    u = _matmul(h, p['w_in'].astype(BF16), BF16)
    ya = _lru_branch(u, p['conv_w'], p['conv_b'], p['lru_w_a'], p['lru_b_a'], p['lru_w_x'], p['lru_b_x'],
                     p['lru_lam'], dims)
    yb = _ret_branch(u, cos, sin, p['ret_decay'], dims)
    yc = _s5_branch(u, p['s5_a_re'], p['s5_a_im'], p['s5_log_dt'], p['s5_b_re'], p['s5_b_im'],
                    p['s5_c_re'], p['s5_c_im'], p['s5_d'], dims)
    yc = _glu(yc, p['s5_w_glu'].astype(BF16), p['s5_b_glu'])
    m = _merge(h, ya, yb, yc, p['w_gate'].astype(BF16), p['b_gate'], p['w_branch'].astype(BF16))
    z = _matmul_residual(m, p['w_out'].astype(BF16), z, mod, 2, dims)
    return _moe_sublayer(z, p['norm_ffn_g'], mod, dims, p['router_w'], p['router_b'],
                         p['moe_w1'].astype(BF16), p['moe_b1'], p['moe_w2'].astype(BF16), p['moe_b2'])


def kernel(x, c, ctx, c_ctx, mod_w_a, mod_w_b, mod_b, norm_mix_g, norm_ffn_g, w_in, conv_w, conv_b, lru_w_a, lru_b_a, lru_w_x, lru_b_x, lru_lam, ret_decay, s5_a_re, s5_a_im, s5_log_dt, s5_b_re, s5_b_im, s5_c_re, s5_c_im, s5_d, s5_w_glu, s5_b_glu, w_branch, w_gate, b_gate, w_out, router_w, router_b, moe_w1, moe_b1, moe_w2, moe_b2, final_norm_g):
    n_batch, l_len, d = x.shape
    c_len = ctx.shape[1]
    t_len = c_len + l_len
    dims = (n_batch, t_len, c_len)
    z = jnp.concatenate([ctx, x], axis=1).reshape(n_batch * t_len, d)
    pad = (-(n_batch + 1)) % SUBLANES
    cc = jnp.concatenate([c, c_ctx[None, :], jnp.zeros((pad, d), F32)], axis=0)
    mod = _modulation(cc, mod_w_a, mod_w_b, mod_b)
    cos, sin = _rope_tables(l_len, RET_HEAD_DIM)
    params = dict(mod=mod, norm_mix_g=norm_mix_g, norm_ffn_g=norm_ffn_g, w_in=w_in, conv_w=conv_w, conv_b=conv_b,
                  lru_w_a=lru_w_a, lru_b_a=lru_b_a, lru_w_x=lru_w_x, lru_b_x=lru_b_x, lru_lam=lru_lam,
                  ret_decay=ret_decay, s5_a_re=s5_a_re, s5_a_im=s5_a_im, s5_log_dt=s5_log_dt, s5_b_re=s5_b_re,
                  s5_b_im=s5_b_im, s5_c_re=s5_c_re, s5_c_im=s5_c_im, s5_d=s5_d, s5_w_glu=s5_w_glu,
                  s5_b_glu=s5_b_glu, w_branch=w_branch, w_gate=w_gate, b_gate=b_gate, w_out=w_out,
                  router_w=router_w, router_b=router_b, moe_w1=moe_w1, moe_b1=moe_b1, moe_w2=moe_w2,
                  moe_b2=moe_b2)

    def body(zc, p):
        return _layer(zc, p, cos, sin, dims), None
    z, _ = lax.scan(body, z, params)
    return _final_norm(z, final_norm_g, dims)
```

```python
import functools
import math

import jax
import jax.numpy as jnp
from jax import lax
from jax.experimental import pallas as pl
from jax.experimental.pallas import tpu as pltpu

F32 = jnp.float32
BF16 = jnp.bfloat16
HIGHEST = lax.Precision.HIGHEST

V7X_VMEM_BYTES = 64 * 1024 * 1024
VMEM_LIMIT = V7X_VMEM_BYTES - 8 * 1024 * 1024
SUBLANES = 8
LANES = 128

N_IN_SPLITS = 7
N_BRANCH = 3
N_MOD = 6
LRU_BLOCKS = 8
LRU_C = 8.0
CONV_W = 4
RET_HEAD_DIM = 128
RET_CHUNK = 128
ROPE_BASE = 10000.0
GRID_W = 64
S5_IN = 16
S5_STATE = 64
TOP_K = 4
SWIGLU_LIMIT = 7.0
SWIGLU_ALPHA = 1.702
NORM_EPS = 1e-6


def _cparams(*sem):
    return pltpu.CompilerParams(dimension_semantics=sem, vmem_limit_bytes=VMEM_LIMIT)


def _row_tile(t, target):
    best = None
    for d in range(16, min(t, target) + 1, 16):
        if t % d == 0:
            best = d
    assert best is not None, (t, target)
    return best


def _col_tile(n, target):
    best = None
    for d in range(LANES, min(n, target) + 1, LANES):
        if n % d == 0:
            best = d
    assert best is not None, (n, target)
    return best


def _softplus(x):
    return jnp.maximum(x, 0.0) + jnp.log1p(jnp.exp(-jnp.abs(x)))


def _is_ctx_rows(i, tm, t_len, c_len):
    row = (i * tm) % t_len + lax.broadcasted_iota(jnp.int32, (tm, 1), 0)
    return row < c_len


def _mod_kernel(cc_ref, wa_ref, wb_ref, b_ref, o_ref):
    cc = cc_ref[...]
    s = cc * jax.nn.sigmoid(cc)
    t = jnp.dot(s, wa_ref[...], precision=HIGHEST, preferred_element_type=F32)
    o_ref[...] = jnp.dot(t, wb_ref[...], precision=HIGHEST, preferred_element_type=F32) + b_ref[...]


def _modulation(cc, mod_w_a, mod_w_b, mod_b):
    depth, d, r = mod_w_a.shape
    rows = cc.shape[0]
    return pl.pallas_call(
        _mod_kernel,
        out_shape=jax.ShapeDtypeStruct((depth, rows, N_MOD * d), F32),
        grid=(depth, N_MOD),
        in_specs=[
            pl.BlockSpec((rows, d), lambda l, j: (0, 0)),
            pl.BlockSpec((None, d, r), lambda l, j: (l, 0, 0)),
            pl.BlockSpec((None, r, d), lambda l, j: (l, 0, j)),
            pl.BlockSpec((None, 1, d), lambda l, j: (l, 0, j)),
        ],
        out_specs=pl.BlockSpec((None, rows, d), lambda l, j: (l, 0, j)),
        compiler_params=_cparams("arbitrary", "arbitrary"),
        name="modulation",
    )(cc, mod_w_a, mod_w_b, mod_b.reshape(depth, 1, N_MOD * d))


NORM_ROWS = 16
NORM_COLS = 4 * LANES


def _inv_rms(z_ref, rows):
    d = z_ref.shape[1]
    ss = None
    for c0 in range(0, d, NORM_COLS):
        x = z_ref[rows, c0:c0 + NORM_COLS]
        ss = x * x if ss is None else ss + x * x
    return lax.rsqrt(jnp.sum(ss, axis=-1, keepdims=True) / d + NORM_EPS)


def _for_row_blocks(tm, body):
    def step(q, carry):
        body(pl.ds(pl.multiple_of(q * NORM_ROWS, NORM_ROWS), NORM_ROWS), q * NORM_ROWS)
        return carry
    lax.fori_loop(0, tm // NORM_ROWS, step, 0)


def _modulated_chunk(z_ref, g_ref, mods, rows, inv, is_ctx, c0):
    scl_ref, shl_ref, scc_ref, shc_ref = mods
    cols = slice(c0, c0 + NORM_COLS)
    y = z_ref[rows, cols] * inv * g_ref[:, cols]
    scale = jnp.where(is_ctx, scc_ref[:, cols], scl_ref[:, cols])
    shift = jnp.where(is_ctx, shc_ref[:, cols], shl_ref[:, cols])
    return y * (1.0 + scale) + shift


def _ctx_rows_of_block(i, tm, t_len, c_len, first_row):
    row = (i * tm) % t_len + first_row + lax.broadcasted_iota(jnp.int32, (NORM_ROWS, 1), 0)
    return row < c_len


def _norm_mod_kernel(z_ref, g_ref, scl_ref, shl_ref, scc_ref, shc_ref, o_ref, *, tm, t_len, c_len):
    i = pl.program_id(0)
    mods = (scl_ref, shl_ref, scc_ref, shc_ref)

    def block(rows, first_row):
        inv = _inv_rms(z_ref, rows)
        is_ctx = _ctx_rows_of_block(i, tm, t_len, c_len, first_row)
        for c0 in range(0, z_ref.shape[1], NORM_COLS):
            h = _modulated_chunk(z_ref, g_ref, mods, rows, inv, is_ctx, c0)
            o_ref[rows, c0:c0 + NORM_COLS] = h.astype(o_ref.dtype)
    _for_row_blocks(tm, block)


HIGH_HALF = 0xFFFF0000


def _pack_halves(lo, hi):
    lo_bits = pltpu.bitcast(lo.astype(BF16).astype(F32), jnp.uint32)
    hi_bits = pltpu.bitcast(hi.astype(BF16).astype(F32), jnp.uint32)
    return (lo_bits >> jnp.uint32(16)) | (hi_bits & jnp.uint32(HIGH_HALF))


def _unpack_bf16_pairs(w):
    return pltpu.bitcast(w << jnp.uint32(16), F32), pltpu.bitcast(w & jnp.uint32(HIGH_HALF), F32)


def _norm_router_kernel(z_ref, g_ref, scl_ref, shl_ref, scc_ref, shc_ref, rw_ref, rb_ref,
                        hp_ref, idx_ref, rank_ref, prob_ref, cnt_ref, cnt_s, hi_s, lo_s, *, tm, t_len, c_len):
    i = pl.program_id(0)
    mods = (scl_ref, shl_ref, scc_ref, shc_ref)
    half = hp_ref.shape[1]

    @pl.when(i == 0)
    def _():
        cnt_s[...] = jnp.zeros(cnt_s.shape, F32)

    def block(rows, first_row):
        inv = _inv_rms(z_ref, rows)
        is_ctx = _ctx_rows_of_block(i, tm, t_len, c_len, first_row)
        for c0 in range(0, half, NORM_COLS):
            pair = []
            for off in (c0, half + c0):
                h = _modulated_chunk(z_ref, g_ref, mods, rows, inv, is_ctx, off)
                top = h.astype(BF16)
                hi_s[rows, off:off + NORM_COLS] = top
                lo_s[rows, off:off + NORM_COLS] = (h - top.astype(F32)).astype(BF16)
                pair.append(h)
            hp_ref[rows, c0:c0 + NORM_COLS] = _pack_halves(pair[0], pair[1])
    _for_row_blocks(tm, block)
    rw = rw_ref[...]
    w_hi = rw.astype(BF16)
    w_lo = (rw - w_hi.astype(F32)).astype(BF16)
    logits = (jnp.dot(hi_s[...], w_hi, preferred_element_type=F32)
              + jnp.dot(lo_s[...], w_hi, preferred_element_type=F32)
              + jnp.dot(hi_s[...], w_lo, preferred_element_type=F32) + rb_ref[...])
    n_exp = logits.shape[1]
    lane = lax.broadcasted_iota(jnp.int32, logits.shape, 1).astype(F32)
    work = logits
    picked = jnp.zeros(logits.shape, F32)
    firsts, vals = [], []
    for k in range(TOP_K):
        m = jnp.max(work, axis=-1, keepdims=True)
        first = jnp.min(jnp.where(work == m, lane, float(n_exp)), axis=-1, keepdims=True)
        hit = lane == first
        picked = jnp.where(hit, 1.0, picked)
        work = jnp.where(hit, -jnp.inf, work)
        firsts.append(first)
        vals.append(m)
    tri = (lax.broadcasted_iota(jnp.int32, (tm, tm), 0) > lax.broadcasted_iota(jnp.int32, (tm, tm), 1))
    before = jnp.dot(jnp.where(tri, 1.0, 0.0).astype(BF16), picked.astype(BF16), preferred_element_type=F32)
    before = before + cnt_s[...]
    ex = [jnp.exp(v - vals[0]) for v in vals]
    denom = ex[0] + ex[1] + ex[2] + ex[3]
    out_lane = lax.broadcasted_iota(jnp.int32, idx_ref.shape, 1)
    idx_out = jnp.zeros(idx_ref.shape, F32)
    rank_out = jnp.zeros(idx_ref.shape, F32)
    prob_out = jnp.zeros(idx_ref.shape, F32)
    for k in range(TOP_K):
        rank_k = jnp.sum(jnp.where(lane == firsts[k], before, 0.0), axis=-1, keepdims=True)
        idx_out = jnp.where(out_lane == k, firsts[k], idx_out)
        rank_out = jnp.where(out_lane == k, rank_k, rank_out)
        prob_out = jnp.where(out_lane == k, ex[k] / denom, prob_out)
    idx_ref[...] = idx_out.astype(jnp.int32)
    rank_ref[...] = rank_out.astype(jnp.int32)
    prob_ref[...] = prob_out
    cnt_s[...] = cnt_s[...] + jnp.sum(picked, axis=0, keepdims=True)
    cnt_ref[...] = cnt_s[...]


def _mod_specs(tm, t_len, n_batch, d, k_scale, k_shift):
    def lat(k):
        return pl.BlockSpec((None, None, 1, d), lambda i, *_: ((i * tm) // t_len, k, 0, 0))

    def ctx(k):
        return pl.BlockSpec((None, None, 1, d), lambda i, *_: (n_batch, k, 0, 0))
    return [lat(k_scale), lat(k_shift), ctx(k_scale), ctx(k_shift)]


def _norm_mod(z, g, mod, k_shift, k_scale, dims):
    n_batch, t_len, c_len = dims
    n, d = z.shape
    tm = _row_tile(t_len, 272)
    row_spec = pl.BlockSpec((tm, d), lambda i: (i, 0))
    return pl.pallas_call(
        functools.partial(_norm_mod_kernel, tm=tm, t_len=t_len, c_len=c_len),
        out_shape=jax.ShapeDtypeStruct((n, d), BF16),
        grid=(n // tm,),
        in_specs=[row_spec, pl.BlockSpec((1, d), lambda i: (0, 0))] + _mod_specs(tm, t_len, n_batch, d, k_scale, k_shift),
        out_specs=row_spec,
        compiler_params=_cparams("parallel"), name="norm_mod",
    )(z, g.reshape(1, d), mod, mod, mod, mod)


def _norm_router(z, g, mod, k_shift, k_scale, dims, rw, rb):
    n_batch, t_len, c_len = dims
    n, d = z.shape
    n_exp = rw.shape[1]
    tm = max(t for t in (LANES, 2 * LANES) if t_len % t == 0)
    row_spec = pl.BlockSpec((tm, d), lambda i: (i, 0))
    lane_spec = pl.BlockSpec((tm, LANES), lambda i: (i, 0))
    return pl.pallas_call(
        functools.partial(_norm_router_kernel, tm=tm, t_len=t_len, c_len=c_len),
        out_shape=(jax.ShapeDtypeStruct((n, d // 2), jnp.uint32), jax.ShapeDtypeStruct((n, LANES), jnp.int32),
                   jax.ShapeDtypeStruct((n, LANES), jnp.int32), jax.ShapeDtypeStruct((n, LANES), F32),
                   jax.ShapeDtypeStruct((1, n_exp), F32)),
        grid=(n // tm,),
        in_specs=[row_spec, pl.BlockSpec((1, d), lambda i: (0, 0))] + _mod_specs(tm, t_len, n_batch, d, k_scale, k_shift)
        + [pl.BlockSpec((d, n_exp), lambda i: (0, 0)), pl.BlockSpec((1, n_exp), lambda i: (0, 0))],
        out_specs=(pl.BlockSpec((tm, d // 2), lambda i: (i, 0)), lane_spec, lane_spec, lane_spec,
                   pl.BlockSpec((1, n_exp), lambda i: (0, 0))),
        scratch_shapes=[pltpu.VMEM((1, n_exp), F32), pltpu.VMEM((tm, d), BF16), pltpu.VMEM((tm, d), BF16)],
        compiler_params=_cparams("arbitrary"), name="norm_router",
    )(z, g.reshape(1, d), mod, mod, mod, mod, rw, rb.reshape(1, n_exp))


def _final_norm_kernel(z_ref, g_ref, o_ref):
    def block(rows, first_row):
        inv = _inv_rms(z_ref, rows)
        for c0 in range(0, z_ref.shape[1], NORM_COLS):
            cols = slice(c0, c0 + NORM_COLS)
            o_ref[rows, cols] = z_ref[rows, cols] * inv * g_ref[:, cols]
    _for_row_blocks(z_ref.shape[0], block)


def _final_norm(z, g, dims):
    n_batch, t_len, c_len = dims
    n, d = z.shape
    l_len = t_len - c_len
    tm = math.gcd(c_len, l_len)
    per_b = l_len // tm
    off = c_len // tm
    return pl.pallas_call(
        _final_norm_kernel,
        out_shape=jax.ShapeDtypeStruct((n_batch * l_len, d), F32),
        grid=(n_batch, per_b),
        in_specs=[pl.BlockSpec((tm, d), lambda b, s: (b * (t_len // tm) + off + s, 0)),
                  pl.BlockSpec((1, d), lambda b, s: (0, 0))],
        out_specs=pl.BlockSpec((tm, d), lambda b, s: (b * per_b + s, 0)),
        compiler_params=_cparams("parallel", "parallel"), name="final_norm",
    )(z, g.reshape(1, d)).reshape(n_batch, l_len, d)


def _mm_kernel(x_ref, w_ref, o_ref):
    o_ref[...] = jnp.dot(x_ref[...], w_ref[...], preferred_element_type=F32).astype(o_ref.dtype)


def _matmul(x, w, out_dtype, tm_target=1088, tn_target=512):
    n, k = x.shape
    n_out = w.shape[1]
    tm = _row_tile(n, tm_target)
    tn = _col_tile(n_out, tn_target)
    return pl.pallas_call(
        _mm_kernel,
        out_shape=jax.ShapeDtypeStruct((n, n_out), out_dtype),
        grid=(n // tm, n_out // tn),
        in_specs=[pl.BlockSpec((tm, k), lambda i, j: (i, 0)),
                  pl.BlockSpec((k, tn), lambda i, j: (0, j))],
        out_specs=pl.BlockSpec((tm, tn), lambda i, j: (i, j)),
        compiler_params=_cparams("parallel", "arbitrary"), name="matmul",
    )(x, w)


def _mm_residual_kernel(x_ref, w_ref, z_ref, gl_ref, gc_ref, o_ref, *, tm, t_len, c_len):
    acc = jnp.dot(x_ref[...], w_ref[...], preferred_element_type=F32)
    gate = jnp.where(_is_ctx_rows(pl.program_id(0), tm, t_len, c_len), gc_ref[...], gl_ref[...])
    o_ref[...] = z_ref[...] + gate * acc


def _matmul_residual(x, w, z, mod, k_gate, dims, tm_target=1088, tn_target=512):
    n_batch, t_len, c_len = dims
    n, k = x.shape
    d = w.shape[1]
    tm = _row_tile(t_len, tm_target)
    tn = _col_tile(d, tn_target)
    return pl.pallas_call(
        functools.partial(_mm_residual_kernel, tm=tm, t_len=t_len, c_len=c_len),
        out_shape=jax.ShapeDtypeStruct((n, d), F32),
        grid=(n // tm, d // tn),
        in_specs=[pl.BlockSpec((tm, k), lambda i, j: (i, 0)),
                  pl.BlockSpec((k, tn), lambda i, j: (0, j)),
                  pl.BlockSpec((tm, tn), lambda i, j: (i, j)),
                  pl.BlockSpec((None, None, 1, tn), lambda i, j: ((i * tm) // t_len, k_gate, 0, j)),
                  pl.BlockSpec((None, None, 1, tn), lambda i, j: (n_batch, k_gate, 0, j))],
        out_specs=pl.BlockSpec((tm, tn), lambda i, j: (i, j)),
        input_output_aliases={2: 0},
        compiler_params=_cparams("parallel", "arbitrary"), name="matmul_residual",
    )(x, w, z, mod, mod)


def _glu_kernel(x_ref, w_ref, b_ref, y_ref, o_ref):
    acc = jnp.dot(x_ref[...], w_ref[...], preferred_element_type=F32) + b_ref[...]
    o_ref[...] = (y_ref[...].astype(F32) * jax.nn.sigmoid(acc)).astype(o_ref.dtype)


def _glu(y, w, b):
    n, k = y.shape
    tm = _row_tile(n, 1088)
    tn = _col_tile(k, 512)
    return pl.pallas_call(
        _glu_kernel,
        out_shape=jax.ShapeDtypeStruct((n, k), BF16),
        grid=(n // tm, k // tn),
        in_specs=[pl.BlockSpec((tm, k), lambda i, j: (i, 0)),
                  pl.BlockSpec((k, tn), lambda i, j: (0, j)),
                  pl.BlockSpec((1, tn), lambda i, j: (0, j)),
                  pl.BlockSpec((tm, tn), lambda i, j: (i, j))],
        out_specs=pl.BlockSpec((tm, tn), lambda i, j: (i, j)),
        compiler_params=_cparams("parallel", "arbitrary"), name="s5_glu",
    )(y, w, b.reshape(1, k), y)


def _merge_kernel(h_ref, ya_ref, yb_ref, yc_ref, wg_ref, bg_ref, wb_ref, o_ref):
    h = h_ref[...]
    acc = None
    for b, y_ref in enumerate((ya_ref, yb_ref, yc_ref)):
        gate = jax.nn.sigmoid(jnp.dot(h, wg_ref[b], preferred_element_type=F32) + bg_ref[b])
        term = gate * jnp.dot(y_ref[...], wb_ref[b], preferred_element_type=F32)
        acc = term if acc is None else acc + term
    o_ref[...] = acc.astype(o_ref.dtype)


def _merge(h, ya, yb, yc, w_gate, b_gate, w_branch, tm_target=544, tn_target=256):
    n, d = h.shape
    w = ya.shape[1]
    tm = _row_tile(n, tm_target)
    tn = _col_tile(d, tn_target)
    y_spec = pl.BlockSpec((tm, w), lambda i, j: (i, 0))
    return pl.pallas_call(
        _merge_kernel,
        out_shape=jax.ShapeDtypeStruct((n, d), BF16),
        grid=(n // tm, d // tn),
        in_specs=[pl.BlockSpec((tm, d), lambda i, j: (i, 0)), y_spec, y_spec, y_spec,
                  pl.BlockSpec((N_BRANCH, d, tn), lambda i, j: (0, 0, j)),
                  pl.BlockSpec((N_BRANCH, 1, tn), lambda i, j: (0, 0, j)),
                  pl.BlockSpec((N_BRANCH, w, tn), lambda i, j: (0, 0, j))],
        out_specs=pl.BlockSpec((tm, tn), lambda i, j: (i, j)),
        compiler_params=_cparams("parallel", "arbitrary"), name="merge",
    )(h, ya, yb, yc, w_gate, b_gate.reshape(N_BRANCH, 1, d), w_branch)


def _static_chunks(start, size, step):
    return [(s, min(step, start + size - s)) for s in range(start, start + size, step)]


def _scan_block_real(a, b, row, reverse):
    for s in (1, 2, 4):
        if reverse:
            keep = row < SUBLANES - s
            shift = SUBLANES - s
        else:
            keep = row >= s
            shift = s
        a_prev = jnp.where(keep, pltpu.roll(a, shift, 0), 1.0)
        b_prev = jnp.where(keep, pltpu.roll(b, shift, 0), 0.0)
        b = a * b_prev + b
        a = a * a_prev
    return a, b


LRU_BLOCKS_PER_STEP = 2


def _lru_kernel(x_ref, g_ref, cw_ref, cb_ref, wa_ref, ba_ref, wx_ref, bx_ref, lam_ref, o_ref,
                xp_s, xc_s, yf_s, yb_s, sa0, sb0, sa1, sb1, ta0, tb0, ta1, tb1, *, c_len, l_len, rb):
    t_len = c_len + l_len
    width = x_ref.shape[1]
    bw = wa_ref.shape[-1]
    cw = cw_ref[...]
    cb = cb_ref[...]
    zeros8 = jnp.zeros((SUBLANES, width), F32)
    for seg0, seg_n in ((0, c_len), (c_len, l_len)):
        xp_s[0:SUBLANES, :] = zeros8
        xp_s[SUBLANES + seg_n:2 * SUBLANES + seg_n, :] = zeros8
        for s0, sn in _static_chunks(0, seg_n, rb):
            xp_s[SUBLANES + s0:SUBLANES + s0 + sn, :] = x_ref[seg0 + s0:seg0 + s0 + sn, :].astype(F32)
        for s0, sn in _static_chunks(0, seg_n, rb):
            acc = cb + xp_s[SUBLANES - 1 + s0:SUBLANES - 1 + s0 + sn, :] * cw[0:1]
            for j in range(1, CONV_W):
                acc = acc + xp_s[SUBLANES - 1 + j + s0:SUBLANES - 1 + j + s0 + sn, :] * cw[j:j + 1]
            xc_s[seg0 + s0:seg0 + s0 + sn, :] = acc

    row = lax.broadcasted_iota(jnp.int32, (SUBLANES, width), 0)
    n_ch = t_len // rb
    c_ch = c_len // rb
    n_sub = rb // SUBLANES
    stages = ((sa0, sb0, sa1, sb1), (ta0, tb0, ta1, tb1))
    sp = [_softplus(-lam_ref[d:d + 1, :]) for d in range(2)]

    def chunk_rows(c):
        return pl.ds(pl.multiple_of(c * rb, rb), rb)

    def chunks_of(v):
        return v, jnp.where(v < c_ch, c_ch - 1 - v, n_ch + c_ch - 1 - v)

    def gates(v, stage):
        real = v < n_ch
        for d, c in enumerate(chunks_of(v)):
            rows = chunk_rows(jnp.where(real, c, 0))
            for k in range(width // bw):
                cols = slice(k * bw, (k + 1) * bw)
                xc = xc_s[rows, cols]
                xb = xc.astype(BF16)
                rr = jax.nn.sigmoid(jnp.dot(xb, wa_ref[d, k], preferred_element_type=F32) + ba_ref[d:d + 1, cols])
                ii = jax.nn.sigmoid(jnp.dot(xb, wx_ref[d, k], preferred_element_type=F32) + bx_ref[d:d + 1, cols])
                a = jnp.exp((-LRU_C) * rr * sp[d][:, cols])
                stage[2 * d][:, cols] = a
                stage[2 * d + 1][:, cols] = jnp.sqrt(1.0 - a * a) * (ii * xc)

    def scan(v, stage, carry):
        h_f, h_b = carry
        real = v < n_ch
        c_f, c_b = chunks_of(v)
        base_f = jnp.where(real, c_f, n_ch) * rb
        base_b = jnp.where(real, c_b, n_ch) * rb
        for blk in range(n_sub):
            lo_f = blk * SUBLANES
            lo_b = (n_sub - 1 - blk) * SUBLANES
            af, bf = _scan_block_real(stage[0][lo_f:lo_f + SUBLANES, :], stage[1][lo_f:lo_f + SUBLANES, :], row, False)
            ab, bb = _scan_block_real(stage[2][lo_b:lo_b + SUBLANES, :], stage[3][lo_b:lo_b + SUBLANES, :], row, True)
            hs_f = af * h_f + bf
            hs_b = ab * h_b + bb
            yf_s[pl.ds(pl.multiple_of(base_f + lo_f, SUBLANES), SUBLANES), :] = hs_f
            yb_s[pl.ds(pl.multiple_of(base_b + lo_b, SUBLANES), SUBLANES), :] = hs_b
            h_f, h_b = hs_f[SUBLANES - 1:SUBLANES, :], hs_b[0:1, :]
        return h_f, h_b

    gates(0, stages[0])

    def step(k, carry):
        for ph in range(2):
            v = 2 * k + ph
            gates(v + 1, stages[(ph + 1) % 2])
            carry = scan(v, stages[ph], carry)
        return carry
    h0 = jnp.zeros((1, width), F32)
    lax.fori_loop(0, (n_ch + 1) // 2, step, (h0, h0))

    def finish(r, carry):
        rows = chunk_rows(r)
        y = yf_s[rows, :] + yb_s[rows, :]
        o_ref[rows, :] = (jax.nn.gelu(g_ref[rows, :].astype(F32)) * y).astype(o_ref.dtype)
        return carry
    lax.fori_loop(0, n_ch, finish, 0)


def _lru_branch(u, conv_w, conv_b, w_a, b_a, w_x, b_x, lam, dims):
    n_batch, t_len, c_len = dims
    l_len = t_len - c_len
    n = u.shape[0]
    width = conv_w.shape[1]
    bw = width // LRU_BLOCKS
    per = LRU_BLOCKS_PER_STEP
    cw = per * bw
    rb = math.gcd(math.gcd(c_len, l_len), 256)
    vec2 = pl.BlockSpec((2, cw), lambda b, k: (0, k))
    wspec = pl.BlockSpec((2, per, bw, bw), lambda b, k: (0, k, 0, 0))
    return pl.pallas_call(
        functools.partial(_lru_kernel, c_len=c_len, l_len=l_len, rb=rb),
        out_shape=jax.ShapeDtypeStruct((n, width), BF16),
        grid=(n_batch, LRU_BLOCKS // per),
        in_specs=[pl.BlockSpec((t_len, cw), lambda b, k: (b, k)),
                  pl.BlockSpec((t_len, cw), lambda b, k: (b, LRU_BLOCKS // per + k)),
                  pl.BlockSpec((CONV_W, cw), lambda b, k: (0, k)),
                  pl.BlockSpec((1, cw), lambda b, k: (0, k)),
                  wspec, vec2, wspec, vec2, vec2],
        out_specs=pl.BlockSpec((t_len, cw), lambda b, k: (b, k)),
        scratch_shapes=[pltpu.VMEM((l_len + 2 * SUBLANES, cw), F32), pltpu.VMEM((t_len, cw), F32),
                        pltpu.VMEM((t_len + rb, cw), F32), pltpu.VMEM((t_len + rb, cw), F32)]
        + [pltpu.VMEM((rb, cw), F32)] * 8,
        compiler_params=_cparams("parallel", "parallel"), name="rglru",
    )(u, u, conv_w, conv_b.reshape(1, width), w_a.astype(BF16), b_a, w_x.astype(BF16), b_x, lam)


RET_HEADS_PER_STEP = 2


def _ret_kernel(q_ref, k_ref, v_ref, g_ref, cos_ref, sin_ref, dec_ref, o_ref,
                qs_s, ks_s, sb_s, dm_s, vec_s, *, c_len, l_len, rb):
    t_len = c_len + l_len
    hd = RET_HEAD_DIM
    heads = q_ref.shape[1] // hd
    ch = RET_CHUNK
    k_scale = hd ** -0.5
    ri = lax.broadcasted_iota(jnp.int32, (ch, ch), 0).astype(F32)
    ci = lax.broadcasted_iota(jnp.int32, (ch, ch), 1).astype(F32)
    diff = ri - ci
    pos = lax.broadcasted_iota(jnp.int32, (ch, hd), 0).astype(F32)
    cd_f, cd_b = [], []
    for h in range(heads):
        dec = dec_ref[:, h]
        lg_f = -_softplus(-dec[0])
        lg_b = -_softplus(-dec[1])
        dm_s[h] = (jnp.where(diff >= 0, jnp.exp(lg_f * jnp.maximum(diff, 0.0)), 0.0)
                   + jnp.where(diff <= 0, jnp.exp(lg_b * jnp.maximum(-diff, 0.0)), 0.0))
        vec_s[h, 0] = jnp.exp(lg_f * (ch - 1.0 - pos))
        vec_s[h, 1] = jnp.exp(lg_f * (pos + 1.0))
        vec_s[h, 2] = jnp.exp(lg_b * pos)
        vec_s[h, 3] = jnp.exp(lg_b * (ch - pos))
        cd_f.append(jnp.exp(lg_f * float(ch)))
        cd_b.append(jnp.exp(lg_b * float(ch)))

    def cols(h):
        return slice(h * hd, (h + 1) * hd)

    for s0, sn in _static_chunks(0, c_len, rb):
        qs_s[s0:s0 + sn, :] = q_ref[s0:s0 + sn, :].astype(F32)
        ks_s[s0:s0 + sn, :] = k_ref[s0:s0 + sn, :].astype(F32) * k_scale
    lane = lax.broadcasted_iota(jnp.int32, (rb, hd), 1)
    low = (lane % (hd // 2)) < (hd // 4)

    def rope(r, carry):
        src = pl.ds(pl.multiple_of(c_len + r * rb, rb), rb)
        tab = pl.ds(pl.multiple_of(r * rb, rb), rb)
        cs = cos_ref[tab, :]
        sn = sin_ref[tab, :]
        for ref, dst, scale in ((q_ref, qs_s, 1.0), (k_ref, ks_s, k_scale)):
            for h in range(heads):
                x = ref[src, cols(h)].astype(F32)
                partner = jnp.where(low, pltpu.roll(x, hd - hd // 4, 1), pltpu.roll(x, hd // 4, 1))
                y = x * cs + partner * sn
                dst[src, cols(h)] = y * scale if scale != 1.0 else y
        return carry
    lax.fori_loop(0, l_len // rb, rope, 0)

    def chunk_rows(c):
        return pl.ds(pl.multiple_of(c * ch, ch), ch)

    def kv_state(rows, h, kdec):
        kd = (ks_s[rows, cols(h)] * kdec).T.astype(BF16)
        return jnp.dot(kd, v_ref[rows, cols(h)], preferred_element_type=F32)

    n_ch = t_len // ch
    c_ch = c_len // ch

    def back(i, states, lo, hi):
        c = hi - 1 - i
        out = []
        for h in range(heads):
            sb_s[h, c] = states[h]
            out.append(cd_b[h] * states[h] + kv_state(chunk_rows(c), h, vec_s[h, 2]))
        return tuple(out)
    zero = tuple(jnp.zeros((hd, hd), F32) for _ in range(heads))
    s_ctx = lax.fori_loop(0, c_ch, functools.partial(back, lo=0, hi=c_ch), zero)
    lax.fori_loop(0, n_ch - c_ch, functools.partial(back, lo=c_ch, hi=n_ch), s_ctx)

    def fwd(c, states):
        rows = chunk_rows(c)
        out = []
        for h in range(heads):
            s = states[h]
            q = qs_s[rows, cols(h)]
            qb = q.astype(BF16)
            kb = ks_s[rows, cols(h)].astype(BF16)
            scores = lax.dot_general(qb, kb, (((1,), (1,)), ((), ())), preferred_element_type=F32) * dm_s[h]
            o = jnp.dot(scores.astype(BF16), v_ref[rows, cols(h)], preferred_element_type=F32)
            o = o + jnp.dot((q * vec_s[h, 1]).astype(BF16), s.astype(BF16), preferred_element_type=F32)
            o = o + jnp.dot((q * vec_s[h, 3]).astype(BF16), sb_s[h, c].astype(BF16), preferred_element_type=F32)
            o = o * lax.rsqrt(jnp.mean(o * o, axis=-1, keepdims=True) + NORM_EPS)
            g = g_ref[rows, cols(h)].astype(F32)
            o_ref[rows, cols(h)] = (o * (g * jax.nn.sigmoid(g))).astype(o_ref.dtype)
            out.append(cd_f[h] * s + kv_state(rows, h, vec_s[h, 0]))
        return tuple(out)
    lax.fori_loop(0, n_ch, fwd, zero)


def _rope_tables(l_len, hd):
    t = jnp.arange(l_len, dtype=jnp.int32)
    row = (t // GRID_W).astype(F32)
    col = (t % GRID_W).astype(F32)
    quarter = hd // 4
    freqs = ROPE_BASE ** (-jnp.arange(quarter, dtype=F32) / quarter)
    ang_r = row[:, None] * freqs
    ang_c = col[:, None] * freqs
    cos = jnp.concatenate([jnp.cos(ang_r)] * 2 + [jnp.cos(ang_c)] * 2, axis=-1)
    sin = jnp.concatenate([-jnp.sin(ang_r), jnp.sin(ang_r), -jnp.sin(ang_c), jnp.sin(ang_c)], axis=-1)
    return cos, sin


def _ret_branch(u, cos, sin, decay, dims):
    n_batch, t_len, c_len = dims
    l_len = t_len - c_len
    n = u.shape[0]
    hd = RET_HEAD_DIM
    n_heads = decay.shape[1]
    rb = math.gcd(math.gcd(c_len, l_len), 256)

    per = RET_HEADS_PER_STEP
    steps = n_heads // per
    wide = per * hd

    def col(split):
        return pl.BlockSpec((t_len, wide), lambda b, h: (b, split * steps + h))
    tab = pl.BlockSpec((l_len, hd), lambda b, h: (0, 0))
    return pl.pallas_call(
        functools.partial(_ret_kernel, c_len=c_len, l_len=l_len, rb=rb),
        out_shape=jax.ShapeDtypeStruct((n, n_heads * hd), BF16),
        grid=(n_batch, steps),
        in_specs=[col(2), col(3), col(4), col(5), tab, tab,
                  pl.BlockSpec((2, per, 1, 1), lambda b, h: (0, h, 0, 0))],
        out_specs=pl.BlockSpec((t_len, wide), lambda b, h: (b, h)),
        scratch_shapes=[pltpu.VMEM((t_len, wide), F32), pltpu.VMEM((t_len, wide), F32),
                        pltpu.VMEM((per, t_len // RET_CHUNK, hd, hd), F32),
                        pltpu.VMEM((per, RET_CHUNK, RET_CHUNK), F32),
                        pltpu.VMEM((per, 4, RET_CHUNK, hd), F32)],
        compiler_params=_cparams("parallel", "parallel"), name="retention",
    )(u, u, u, u, cos, sin, decay.reshape(2, n_heads, 1, 1))


def _s5_disc_kernel(are_ref, aim_ref, ldt_ref, bre_ref, bim_ref, pre_ref, pim_ref, bbre_ref, bbim_ref):
    lam_re = jnp.minimum(are_ref[...], -1e-4)
    lam_im = aim_ref[...]
    dt = jnp.exp(ldt_ref[...])
    z_re = lam_re * dt
    z_im = lam_im * dt
    mag = jnp.exp(z_re)
    ab_re = mag * jnp.cos(z_im)
    ab_im = mag * jnp.sin(z_im)
    den = lam_re * lam_re + lam_im * lam_im
    n_re = ab_re - 1.0
    co_re = (n_re * lam_re + ab_im * lam_im) / den
    co_im = (ab_im * lam_re - n_re * lam_im) / den
    for i in range(bre_ref.shape[0]):
        b_re = bre_ref[i]
        b_im = bim_ref[i]
        bbre_ref[i] = co_re * b_re - co_im * b_im
        bbim_ref[i] = co_re * b_im + co_im * b_re
    p_re, p_im = ab_re, ab_im
    pre_ref[0] = p_re
    pim_ref[0] = p_im
    for j in range(1, SUBLANES):
        p_re, p_im = p_re * ab_re - p_im * ab_im, p_re * ab_im + p_im * ab_re
        pre_ref[j] = p_re
        pim_ref[j] = p_im


def _s5_discretise(a_re, a_im, log_dt, b_re, b_im):
    two, g, p = a_re.shape
    i = b_re.shape[-1]
    rows = two * g
    full2 = pl.BlockSpec((rows, p), lambda: (0, 0))
    full3 = pl.BlockSpec((i, rows, p), lambda: (0, 0, 0))
    pw = pl.BlockSpec((SUBLANES, rows, p), lambda: (0, 0, 0))

    def input_major(m):
        return jnp.transpose(m, (3, 0, 1, 2)).reshape(i, rows, p)
    pw_re, pw_im, bb_re, bb_im = pl.pallas_call(
        _s5_disc_kernel,
        out_shape=(jax.ShapeDtypeStruct((SUBLANES, rows, p), F32), jax.ShapeDtypeStruct((SUBLANES, rows, p), F32),
                   jax.ShapeDtypeStruct((i, rows, p), F32), jax.ShapeDtypeStruct((i, rows, p), F32)),
        in_specs=[full2, full2, pl.BlockSpec((rows, 1), lambda: (0, 0)), full3, full3],
        out_specs=(pw, pw, full3, full3),
        name="s5_discretise",
    )(a_re.reshape(rows, p), a_im.reshape(rows, p), log_dt.reshape(rows, 1), input_major(b_re), input_major(b_im))
    return pw_re, pw_im, jnp.swapaxes(bb_re, 0, 1), jnp.swapaxes(bb_im, 0, 1)


def _s5_kernel(u_ref, bb_ref, cb_ref, pw_ref, dsk_ref, o_ref, buf0, buf1, buf2, y_s, lv_s, *, c_len, l_len, rb):
    t_len = c_len + l_len
    ns = buf0.shape[1] // 2
    bufs = (buf0, buf1, buf2)
    row = lax.broadcasted_iota(jnp.int32, (SUBLANES, ns), 0)
    dsk = dsk_ref[...]

    def skip(r, carry):
        rows = pl.ds(pl.multiple_of(r * rb, rb), rb)
        y_s[rows, :] = dsk * u_ref[rows, :].astype(F32)
        return carry
    lax.fori_loop(0, t_len // rb, skip, 0)

    y_s[pl.ds(t_len, rb), :] = jnp.zeros((rb, y_s.shape[1]), F32)
    buf2[...] = jnp.zeros(buf2.shape, F32)
    n_ch = t_len // rb
    c_ch = c_len // rb
    n_iter = (n_ch + 3) // 3

    def chunk_rows(c):
        return pl.ds(c * rb, rb) if isinstance(c, int) else pl.ds(pl.multiple_of(c * rb, rb), rb)

    for d in range(2):
        reverse = d == 1
        bb = bb_ref[d]
        cb = cb_ref[d]

        for lvl, s in enumerate((1, 2, 4)):
            keep = (row < SUBLANES - s) if reverse else (row >= s)
            for part in range(2):
                lv_s[lvl, part] = jnp.where(keep, pw_ref[d, part, s - 1:s, :], 0.0)
        for part in range(2):
            if reverse:
                for j in range(SUBLANES):
                    lv_s[3, part, j:j + 1, :] = pw_ref[d, part, SUBLANES - 1 - j:SUBLANES - j, :]
            else:
                lv_s[3, part] = pw_ref[d, part]

        def chunk_of(i, reverse=reverse):
            if not reverse:
                return i
            return jnp.where(i < c_ch, c_ch - 1 - i, n_ch + c_ch - 1 - i)

        def drive(i, buf, bb=bb):
            src = jnp.where(i < n_ch, chunk_of(i), 0)
            buf[...] = jnp.dot(u_ref[chunk_rows(src), :], bb, preferred_element_type=F32)

        def readout(i, buf, cb=cb):
            dst = jnp.where(jnp.logical_and(i >= 0, i < n_ch), chunk_of(i), n_ch)
            rows = chunk_rows(dst)
            y = jnp.dot(buf[:, :ns].astype(BF16), cb[:ns, :], preferred_element_type=F32)
            y = y + jnp.dot(buf[:, ns:].astype(BF16), cb[ns:, :], preferred_element_type=F32)
            y_s[rows, :] = y_s[rows, :] + y

        def scan_chunk(buf, carry, reverse=reverse):
            cr, ci = carry
            order = range(rb // SUBLANES - 1, -1, -1) if reverse else range(rb // SUBLANES)
            for blk in order:
                rows = slice(blk * SUBLANES, (blk + 1) * SUBLANES)
                hr = buf[rows, :ns]
                hi_ = buf[rows, ns:]
                for lvl, s in enumerate((1, 2, 4)):
                    shift = SUBLANES - s if reverse else s
                    ar = lv_s[lvl, 0]
                    ai = lv_s[lvl, 1]
                    sr = pltpu.roll(hr, shift, 0)
                    si = pltpu.roll(hi_, shift, 0)
                    hr, hi_ = hr + (ar * sr - ai * si), hi_ + (ar * si + ai * sr)
                pr = lv_s[3, 0]
                pi = lv_s[3, 1]
                hr, hi_ = hr + (pr * cr - pi * ci), hi_ + (pr * ci + pi * cr)
                buf[rows, :ns] = hr
                buf[rows, ns:] = hi_
                if reverse:
                    cr, ci = hr[0:1, :], hi_[0:1, :]
                else:
                    cr, ci = hr[SUBLANES - 1:SUBLANES, :], hi_[SUBLANES - 1:SUBLANES, :]
            return cr, ci

        drive(0, bufs[0])

        def step(k, carry):
            for ph in range(3):
                i = 3 * k + ph
                drive(i + 1, bufs[(ph + 1) % 3])
                carry = scan_chunk(bufs[ph], carry)
                readout(i - 1, bufs[(ph + 2) % 3])
            return carry
        zero = (jnp.zeros((1, ns), F32), jnp.zeros((1, ns), F32))
        lax.fori_loop(0, n_iter, step, zero)

    def finish(r, carry):
        rows = pl.ds(pl.multiple_of(r * rb, rb), rb)
        o_ref[rows, :] = jax.nn.gelu(y_s[rows, :]).astype(o_ref.dtype)
        return carry
    lax.fori_loop(0, t_len // rb, finish, 0)


def _s5_branch(u, a_re, a_im, log_dt, b_re, b_im, c_re, c_im, d_skip, dims):
    n_batch, t_len, c_len = dims
    l_len = t_len - c_len
    n = u.shape[0]
    _, g, p = a_re.shape
    i = b_re.shape[-1]
    width = g * i
    gpb = LANES // i
    nb = g // gpb
    ns = gpb * p
    pw_re, pw_im, bb_re, bb_im = _s5_discretise(a_re, a_im, log_dt, b_re, b_im)
    on_diag = (jnp.arange(gpb * i)[:, None] // i) == (jnp.arange(ns)[None, :] // p)

    def blockdiag_in(m):
        m = m.reshape(2, nb, gpb * i, p)
        return jnp.where(on_diag, jnp.tile(m, (1, 1, 1, gpb)), 0.0)
    bb = jnp.concatenate([blockdiag_in(bb_re), blockdiag_in(bb_im)], axis=-1).astype(BF16)

    def blockdiag_out(m):
        m = jnp.swapaxes(m.reshape(2, nb, gpb * i, p), -1, -2)
        return jnp.where(on_diag.T, jnp.tile(m, (1, 1, gpb, 1)), 0.0)
    cb = jnp.concatenate([blockdiag_out(c_re), -blockdiag_out(c_im)], axis=-2).astype(BF16)
    pw = jnp.stack([pw_re.reshape(SUBLANES, 2, g * p), pw_im.reshape(SUBLANES, 2, g * p)], axis=0)
    pw = jnp.transpose(pw, (2, 0, 1, 3))
    rb = math.gcd(math.gcd(c_len, l_len), 256)
    cols = (N_IN_SPLITS - 1) * width // LANES
    return pl.pallas_call(
        functools.partial(_s5_kernel, c_len=c_len, l_len=l_len, rb=rb),
        out_shape=jax.ShapeDtypeStruct((n, width), BF16),
        grid=(n_batch, nb),
        in_specs=[pl.BlockSpec((t_len, LANES), lambda b, k: (b, cols + k)),
                  pl.BlockSpec((2, None, LANES, 2 * ns), lambda b, k: (0, k, 0, 0)),
                  pl.BlockSpec((2, None, 2 * ns, LANES), lambda b, k: (0, k, 0, 0)),
                  pl.BlockSpec((2, 2, SUBLANES, ns), lambda b, k: (0, 0, 0, k)),
                  pl.BlockSpec((1, LANES), lambda b, k: (0, k))],
        out_specs=pl.BlockSpec((t_len, LANES), lambda b, k: (b, k)),
        scratch_shapes=[pltpu.VMEM((rb, 2 * ns), F32), pltpu.VMEM((rb, 2 * ns), F32), pltpu.VMEM((rb, 2 * ns), F32),
                        pltpu.VMEM((t_len + rb, LANES), F32),
                        pltpu.VMEM((4, 2, SUBLANES, ns), F32)],
        compiler_params=_cparams("parallel", "parallel"), name="s5",
    )(u, bb, cb, pw, d_skip.reshape(1, width))


EXPERT_TILE = 256
EXPERT_LOOKAHEAD = 2
GATHER_UNROLL = 8


def _expert_kernel(te_ref, nu_ref, tok_ref, hp_hbm, w1_ref, b1_ref, w2_ref, b2_ref, y_ref, xbuf, xb, act_s, sem,
                   *, tme, ff, dc):
    j = pl.program_id(0)
    n_used = nu_ref[0]
    n_buf = xbuf.shape[0]
    ahead = n_buf - 1
    slot = j % n_buf
    half = xbuf.shape[2]

    def row_copy(tile, r):
        tok = tok_ref[tile * tme + r]
        s = tile % n_buf
        return pltpu.make_async_copy(hp_hbm.at[pl.ds(tok, 1), :], xbuf.at[s, pl.ds(r, 1), :], sem.at[s])

    def start_tile(tile):
        def body(r, c):
            row_copy(tile, r).start()
            return c
        lax.fori_loop(0, tme, body, 0, unroll=GATHER_UNROLL)

    @pl.when(j == 0)
    def _():
        for t in range(ahead):
            start_tile(t)

    def wait_tile(tile):
        def body(r, c):
            row_copy(tile, r).wait()
            return c
        lax.fori_loop(0, tme, body, 0, unroll=GATHER_UNROLL)

    @pl.when(j < n_used)
    def _():
        wait_tile(j)
        rc = 2 * SUBLANES

        def unpack(q, c):
            rows = pl.ds(pl.multiple_of(q * rc, rc), rc)
            for c0 in range(0, half, COMBINE_COLS):
                lo, hi = _unpack_bf16_pairs(xbuf[slot, rows, c0:c0 + COMBINE_COLS])
                xb[rows, c0:c0 + COMBINE_COLS] = lo.astype(BF16)
                xb[rows, half + c0:half + c0 + COMBINE_COLS] = hi.astype(BF16)
            return c
        lax.fori_loop(0, tme // rc, unpack, 0)
        for r in range(tme):
            row_copy(j + ahead, r).start()
        hu = jnp.dot(xb[...], w1_ref[...], preferred_element_type=F32) + b1_ref[...]
        gate = jnp.minimum(hu[:, :ff], SWIGLU_LIMIT)
        up = jnp.clip(hu[:, ff:], -SWIGLU_LIMIT, SWIGLU_LIMIT)
        act_s[...] = (gate * jax.nn.sigmoid(SWIGLU_ALPHA * gate) * (up + 1.0)).astype(BF16)
        for c0 in range(0, half, dc):
            lo = jnp.dot(act_s[...], w2_ref[:, c0:c0 + dc], preferred_element_type=F32) + b2_ref[:, c0:c0 + dc]
            hi = (jnp.dot(act_s[...], w2_ref[:, half + c0:half + c0 + dc], preferred_element_type=F32)
                  + b2_ref[:, half + c0:half + c0 + dc])
            y_ref[:, c0:c0 + dc] = _pack_halves(lo, hi)

    @pl.when(jnp.logical_and(j >= n_used, j < n_used + ahead))
    def _():
        wait_tile(j)

    @pl.when(j >= n_used)
    def _():
        y_ref[...] = jnp.zeros(y_ref.shape, y_ref.dtype)


def _experts(hp, tile_expert, n_used, tok_of_slot, w1, b1, w2, b2):
    n_exp, d, ff2 = w1.shape
    half = d // 2
    n_tiles = tile_expert.shape[0]
    tme = EXPERT_TILE
    grid_spec = pltpu.PrefetchScalarGridSpec(
        num_scalar_prefetch=3,
        grid=(n_tiles,),
        in_specs=[pl.BlockSpec(memory_space=pl.ANY),
                  pl.BlockSpec((None, d, ff2), lambda j, te, nu, tok: (te[j], 0, 0)),
                  pl.BlockSpec((None, 1, ff2), lambda j, te, nu, tok: (te[j], 0, 0)),
                  pl.BlockSpec((None, ff2 // 2, d), lambda j, te, nu, tok: (te[j], 0, 0)),
                  pl.BlockSpec((None, 1, d), lambda j, te, nu, tok: (te[j], 0, 0))],
        out_specs=pl.BlockSpec((tme, half), lambda j, te, nu, tok: (j, 0)),
        scratch_shapes=[pltpu.VMEM((EXPERT_LOOKAHEAD + 1, tme, half), jnp.uint32), pltpu.VMEM((tme, d), BF16),
                        pltpu.VMEM((tme, ff2 // 2), BF16),
                        pltpu.SemaphoreType.DMA((EXPERT_LOOKAHEAD + 1,))],
    )
    return pl.pallas_call(
        functools.partial(_expert_kernel, tme=tme, ff=ff2 // 2, dc=_col_tile(half, 512)),
        out_shape=jax.ShapeDtypeStruct((n_tiles * tme, half), jnp.uint32),
        grid_spec=grid_spec,
        compiler_params=_cparams("arbitrary"), name="experts",
    )(tile_expert, n_used, tok_of_slot, hp, w1, b1.reshape(n_exp, 1, ff2), w2, b2.reshape(n_exp, 1, d))


COMBINE_ROWS = SUBLANES
COMBINE_COLS = 4 * LANES


def _combine_kernel(pos_ref, y_hbm, prob_ref, z_ref, gl_ref, gc_ref, o_ref, ybuf, sem, *, tm, t_len, c_len):
    i = pl.program_id(0)
    n_steps = pl.num_programs(0)
    slot = i % 2
    half = ybuf.shape[3]

    def row_copy(tile, r, k, s):
        row = pos_ref[(tile * tm + r) * TOP_K + k]
        return pltpu.make_async_copy(y_hbm.at[pl.ds(row, 1), :], ybuf.at[s, k, pl.ds(r, 1), :], sem.at[s])

    def start_tile(tile, s):
        def body(r, c):
            for k in range(TOP_K):
                row_copy(tile, r, k, s).start()
            return c
        lax.fori_loop(0, tm, body, 0, unroll=GATHER_UNROLL // TOP_K)

    @pl.when(i == 0)
    def _():
        start_tile(i, slot)

    def wait_tile(tile, s):
        def body(r, c):
            for k in range(TOP_K):
                row_copy(tile, r, k, s).wait()
            return c
        lax.fori_loop(0, tm, body, 0, unroll=GATHER_UNROLL // TOP_K)
    wait_tile(i, slot)

    rc = COMBINE_ROWS
    cw = COMBINE_COLS
    nxt = jnp.minimum(i + 1, n_steps - 1)

    def reduce_rows(q, c):
        for r in range(rc):
            for k in range(TOP_K):
                row_copy(nxt, q * rc + r, k, 1 - slot).start()
        rows = pl.ds(q * rc, rc)
        prob = prob_ref[rows, :]
        row = (i * tm) % t_len + q * rc + lax.broadcasted_iota(jnp.int32, (rc, 1), 0)
        is_ctx = row < c_len
        for c0 in range(0, half, cw):
            acc_lo = acc_hi = None
            for k in range(TOP_K):
                lo, hi = _unpack_bf16_pairs(ybuf[slot, k, rows, c0:c0 + cw])
                p = prob[:, k:k + 1]
                acc_lo = p * lo if acc_lo is None else acc_lo + p * lo
                acc_hi = p * hi if acc_hi is None else acc_hi + p * hi
            for acc, off in ((acc_lo, c0), (acc_hi, half + c0)):
                gate = jnp.where(is_ctx, gc_ref[:, off:off + cw], gl_ref[:, off:off + cw])
                o_ref[rows, off:off + cw] = z_ref[rows, off:off + cw] + gate * acc
        return c
    for q in range(tm // rc):
        reduce_rows(q, 0)

    @pl.when(i == n_steps - 1)
    def _():
        wait_tile(nxt, 1 - slot)


def _combine(y_sorted, pos, prob, z, mod, k_gate, dims):
    n_batch, t_len, c_len = dims
    n, d = z.shape
    half = d // 2
    tm = max(t for t in (LANES, 2 * LANES) if t_len % t == 0)
    grid_spec = pltpu.PrefetchScalarGridSpec(
        num_scalar_prefetch=1,
        grid=(n // tm,),
        in_specs=[pl.BlockSpec(memory_space=pl.ANY),
                  pl.BlockSpec((tm, LANES), lambda i, pos: (i, 0)),
                  pl.BlockSpec((tm, d), lambda i, pos: (i, 0)),
                  pl.BlockSpec((None, None, 1, d), lambda i, pos: ((i * tm) // t_len, k_gate, 0, 0)),
                  pl.BlockSpec((None, None, 1, d), lambda i, pos: (n_batch, k_gate, 0, 0))],
        out_specs=pl.BlockSpec((tm, d), lambda i, pos: (i, 0)),
        scratch_shapes=[pltpu.VMEM((2, TOP_K, tm, half), jnp.uint32), pltpu.SemaphoreType.DMA((2,))],
    )
    return pl.pallas_call(
        functools.partial(_combine_kernel, tm=tm, t_len=t_len, c_len=c_len),
        out_shape=jax.ShapeDtypeStruct((n, d), F32),
        grid_spec=grid_spec,
        input_output_aliases={3: 0},
        compiler_params=_cparams("arbitrary"), name="moe_combine",
    )(pos, y_sorted, prob, z, mod, mod)


def _route(idx, rank, counts, n_exp):
    n = idx.shape[0]
    tme = EXPERT_TILE
    n_tiles = (n * TOP_K + n_exp * (tme - 1)) // tme + EXPERT_LOOKAHEAD
    counts = counts.reshape(n_exp).astype(jnp.int32)
    padded = ((counts + tme - 1) // tme) * tme
    ends = jnp.cumsum(padded)
    base = ends - padded
    idx = idx[:, :TOP_K]
    pos = jnp.take(base, idx) + rank[:, :TOP_K]
    tile_start = jnp.arange(n_tiles, dtype=jnp.int32) * tme
    tile_expert = jnp.sum((ends[None, :] <= tile_start[:, None]).astype(jnp.int32), axis=1)
    tile_expert = jnp.minimum(tile_expert, n_exp - 1)
    n_used = (ends[-1:] // tme).astype(jnp.int32)
    token = jnp.broadcast_to(jnp.arange(n, dtype=jnp.int32)[:, None], (n, TOP_K))
    tok_of_slot = jnp.zeros((n_tiles * tme,), jnp.int32).at[pos.reshape(-1)].set(
        token.reshape(-1), unique_indices=True, indices_are_sorted=False)
    return pos.reshape(-1).astype(jnp.int32), tile_expert, n_used, tok_of_slot


def _moe_sublayer(z, g, mod, dims, rw, rb, w1, b1, w2, b2):
    hp, idx, rank, prob, counts = _norm_router(z, g, mod, 3, 4, dims, rw, rb)
    pos, tile_expert, n_used, tok_of_slot = _route(idx, rank, counts, rw.shape[1])
    y_sorted = _experts(hp, tile_expert, n_used, tok_of_slot, w1, b1, w2, b2)
    return _combine(y_sorted, pos, prob, z, mod, 5, dims)


def _layer(z, p, cos, sin, dims):
    d = z.shape[1]
    mod = p['mod'].reshape(p['mod'].shape[0], N_MOD, 1, d)
    h = _norm_mod(z, p['norm_mix_g'], mod, 0, 1, dims)
    u = _matmul(h, p['w_in'].astype(BF16), BF16)
    ya = _lru_branch(u, p['conv_w'], p['conv_b'], p['lru_w_a'], p['lru_b_a'], p['lru_w_x'], p['lru_b_x'],
                     p['lru_lam'], dims)
    yb = _ret_branch(u, cos, sin, p['ret_decay'], dims)
    yc = _s5_branch(u, p['s5_a_re'], p['s5_a_im'], p['s5_log_dt'], p['s5_b_re'], p['s5_b_im'],
                    p['s5_c_re'], p['s5_c_im'], p['s5_d'], dims)
    yc = _glu(yc, p['s5_w_glu'].astype(BF16), p['s5_b_glu'])
    m = _merge(h, ya, yb, yc, p['w_gate'].astype(BF16), p['b_gate'], p['w_branch'].astype(BF16))
    z = _matmul_residual(m, p['w_out'].astype(BF16), z, mod, 2, dims)
    return _moe_sublayer(z, p['norm_ffn_g'], mod, dims, p['router_w'], p['router_b'],
                         p['moe_w1'].astype(BF16), p['moe_b1'], p['moe_w2'].astype(BF16), p['moe_b2'])


def kernel(x, c, ctx, c_ctx, mod_w_a, mod_w_b, mod_b, norm_mix_g, norm_ffn_g, w_in, conv_w, conv_b, lru_w_a, lru_b_a, lru_w_x, lru_b_x, lru_lam, ret_decay, s5_a_re, s5_a_im, s5_log_dt, s5_b_re, s5_b_im, s5_c_re, s5_c_im, s5_d, s5_w_glu, s5_b_glu, w_branch, w_gate, b_gate, w_out, router_w, router_b, moe_w1, moe_b1, moe_w2, moe_b2, final_norm_g):
    n_batch, l_len, d = x.shape
    c_len = ctx.shape[1]
    t_len = c_len + l_len
    dims = (n_batch, t_len, c_len)
    z = jnp.concatenate([ctx, x], axis=1).reshape(n_batch * t_len, d)
    pad = (-(n_batch + 1)) % SUBLANES
    cc = jnp.concatenate([c, c_ctx[None, :], jnp.zeros((pad, d), F32)], axis=0)
    mod = _modulation(cc, mod_w_a, mod_w_b, mod_b)
    cos, sin = _rope_tables(l_len, RET_HEAD_DIM)
    params = dict(mod=mod, norm_mix_g=norm_mix_g, norm_ffn_g=norm_ffn_g, w_in=w_in, conv_w=conv_w, conv_b=conv_b,
                  lru_w_a=lru_w_a, lru_b_a=lru_b_a, lru_w_x=lru_w_x, lru_b_x=lru_b_x, lru_lam=lru_lam,
                  ret_decay=ret_decay, s5_a_re=s5_a_re, s5_a_im=s5_a_im, s5_log_dt=s5_log_dt, s5_b_re=s5_b_re,
                  s5_b_im=s5_b_im, s5_c_re=s5_c_re, s5_c_im=s5_c_im, s5_d=s5_d, s5_w_glu=s5_w_glu,
                  s5_b_glu=s5_b_glu, w_branch=w_branch, w_gate=w_gate, b_gate=b_gate, w_out=w_out,
                  router_w=router_w, router_b=router_b, moe_w1=moe_w1, moe_b1=moe_b1, moe_w2=moe_w2,
                  moe_b2=moe_b2)

    def body(zc, p):
        return _layer(zc, p, cos, sin, dims), None
    z, _ = lax.scan(body, z, params)
    return _final_norm(z, final_norm_g, dims)
```
